```python
import math
import jax, jax.numpy as jnp
from jax import lax
import numpy as np

D_MODEL = 1024
BATCH = 1
SEQ = 16384
DEPTH = 1

CHUNK = 64
Q_BLOCK = 128
EPS = 1e-6
FOX_HEADS = 8
FOX_HD = 64
FOX_W = FOX_HEADS * FOX_HD
MLA_HEADS = 8
MLA_NOPE = 64
MLA_ROPE = 32
MLA_V = 64
Q_LORA = 256
KV_LORA = 128
ROPE_THETA = 10000.0
MLA_W = MLA_HEADS * MLA_V
MIX_W = FOX_W + MLA_W
IN_SIZES = (FOX_W, FOX_W, FOX_W, FOX_HEADS, Q_LORA, KV_LORA, MLA_ROPE)
IN_COLS = int(sum(IN_SIZES))
IN_SPLITS = [int(s) for s in np.cumsum(IN_SIZES)[:-1]]
D_FF = 4 * D_MODEL

kernel_name = "hymba_fox_mla_sqrelu_block"


def _rmsnorm(x, g):
    xf = x.astype(jnp.float32)
    y = xf * lax.rsqrt(jnp.mean(xf * xf, axis=-1, keepdims=True) + EPS)
    return (y * g.astype(jnp.float32)).astype(x.dtype)


def _rope(x, pos):
    half = x.shape[-1] // 2
    inv = ROPE_THETA ** (-jnp.arange(half, dtype=jnp.float32) / half)
    ang = pos.astype(jnp.float32)[:, None] * inv[None, :]
    cos = jnp.cos(ang).astype(x.dtype)
    sin = jnp.sin(ang).astype(x.dtype)
    x1, x2 = x[..., :half], x[..., half:]
    return jnp.concatenate([x1 * cos - x2 * sin, x2 * cos + x1 * sin], axis=-1)


def _block_sweep(q, k, v, block_bias, scale):
    b, h, s, dk = q.shape
    nb = s // Q_BLOCK
    qb = q.reshape(b, h, nb, Q_BLOCK, dk).transpose(2, 0, 1, 3, 4)

    def one(args):
        i, qi = args
        logits = jnp.einsum('bhqd,bhkd->bhqk', qi, k).astype(jnp.float32) * scale
        probs = jax.nn.softmax(logits + block_bias(i), axis=-1)
        return jnp.einsum('bhqk,bhkd->bhqd', probs.astype(v.dtype), v)

    out = lax.map(one, (jnp.arange(nb), qb))
    return out.transpose(1, 2, 0, 3, 4).reshape(b, h, s, v.shape[-1])


def _hybrid_mixer(h, pos, w_in, b_f, g_q, w_q_up, g_kv, w_kv_up, w_o):
    b, s, _ = h.shape
    p = jnp.einsum('bsd,dc->bsc', h, w_in)
    fq, fk, fv, f_logit, q_lat, kv_lat, k_rope_raw = jnp.split(p, IN_SPLITS, axis=-1)
    kpos = jnp.arange(s)

    def heads(t, nh):
        return t.reshape(b, s, nh, -1).transpose(0, 2, 1, 3)
    fq, fk, fv = heads(fq, FOX_HEADS), heads(fk, FOX_HEADS), heads(fv, FOX_HEADS)
    log_f = jax.nn.log_sigmoid((f_logit + b_f).astype(jnp.float32))
    logcum = jnp.cumsum(log_f, axis=1).transpose(0, 2, 1)

    def fox_bias(i):
        start = i * Q_BLOCK
        cq = lax.dynamic_slice_in_dim(logcum, start, Q_BLOCK, axis=2)
        qpos = start + jnp.arange(Q_BLOCK)
        mask = kpos[None, :] <= qpos[:, None]
        return jnp.where(mask, cq[..., None] - logcum[:, :, None, :], -jnp.inf)

    o_fox = _block_sweep(fq, fk, fv, fox_bias, 1.0 / math.sqrt(FOX_HD))

    cq = jnp.einsum('bsr,rc->bsc', _rmsnorm(q_lat, g_q), w_q_up)
    cq = heads(cq, MLA_HEADS)
    q_mla = jnp.concatenate([cq[..., :MLA_NOPE], _rope(cq[..., MLA_NOPE:], pos)], axis=-1)
    ckv = jnp.einsum('bsr,rc->bsc', _rmsnorm(kv_lat, g_kv), w_kv_up)
    ckv = heads(ckv, MLA_HEADS)
    k_nope, v_mla = ckv[..., :MLA_NOPE], ckv[..., MLA_NOPE:]
    k_rope = _rope(k_rope_raw, pos)[:, None]
    k_mla = jnp.concatenate(
        [k_nope, jnp.broadcast_to(k_rope, (b, MLA_HEADS, s, MLA_ROPE))], axis=-1)

    def chunk_bias(i):
        qpos = i * Q_BLOCK + jnp.arange(Q_BLOCK)
        mask = (kpos // CHUNK)[None, :] <= (qpos // CHUNK)[:, None]
        return jnp.where(mask, jnp.float32(0.0), jnp.float32(-jnp.inf))[None, None]

    o_mla = _block_sweep(q_mla, k_mla, v_mla, chunk_bias,
                         1.0 / math.sqrt(MLA_NOPE + MLA_ROPE))

    o = jnp.concatenate([o_fox, o_mla], axis=1)
    o = o.transpose(0, 2, 1, 3).reshape(b, s, MIX_W)
    return jnp.einsum('bsc,cd->bsd', o, w_o)


def setup_inputs(seed: int = 0) -> dict:
    key = jax.random.key(seed)
    ks = jax.random.split(key, 16)
    L = DEPTH

    def w(k, shape, fan_in):
        return jax.random.normal(k, shape, jnp.float32) * fan_in ** -0.5

    def gain(k, shape):
        return 1.0 + 0.05 * jax.random.normal(k, shape, jnp.float32)

    return {
        "x": jax.random.normal(ks[0], (BATCH, SEQ, D_MODEL), jnp.float32),
        "g_mix": gain(ks[1], (L, D_MODEL)),
        "w_in": w(ks[2], (L, D_MODEL, IN_COLS), D_MODEL),
        "b_f": 3.0 + 0.5 * jax.random.normal(ks[3], (L, FOX_HEADS), jnp.float32),
        "g_q": gain(ks[4], (L, Q_LORA)),
        "w_q_up": w(ks[5], (L, Q_LORA, MLA_HEADS * (MLA_NOPE + MLA_ROPE)), Q_LORA),
        "g_kv": gain(ks[6], (L, KV_LORA)),
        "w_kv_up": w(ks[7], (L, KV_LORA, MLA_HEADS * (MLA_NOPE + MLA_V)), KV_LORA),
        "w_o": w(ks[8], (L, MIX_W, D_MODEL), MIX_W),
        "g_mlp": gain(ks[9], (L, D_MODEL)),
        "w_ff1": w(ks[10], (L, D_MODEL, D_FF), D_MODEL),
        "w_ff2": w(ks[11], (L, D_FF, D_MODEL), D_FF),
        "g_final": gain(ks[12], (D_MODEL,)),
    }


def reference(x, g_mix, w_in, b_f, g_q, w_q_up, g_kv, w_kv_up, w_o,
              g_mlp, w_ff1, w_ff2, g_final):
    pos = jnp.arange(x.shape[1])
    for l in range(DEPTH):
        h = _rmsnorm(x, g_mix[l])
        x = x + _hybrid_mixer(h, pos, w_in[l], b_f[l], g_q[l], w_q_up[l],
                              g_kv[l], w_kv_up[l], w_o[l])
        h2 = _rmsnorm(x, g_mlp[l])
        u = jnp.einsum('bsd,df->bsf', h2, w_ff1[l])
        x = x + jnp.einsum('bsf,fd->bsd', jnp.square(jax.nn.relu(u)), w_ff2[l])
    return _rmsnorm(x, g_final)
```

```python
import functools
import math

import numpy as np
import jax
import jax.numpy as jnp
from jax import lax
from jax.experimental import pallas as pl
from jax.experimental.pallas import tpu as pltpu

F32 = jnp.float32
BF16 = jnp.bfloat16

EPS = 1e-6
CHUNK = 64
N_HEADS = 8
HEAD_DIM = 64
ROPE_DIM = 32
Q_LORA = 256
KV_LORA = 128
ROPE_THETA = 10000.0
FOX_SCALE = 1.0 / math.sqrt(HEAD_DIM)
MLA_SCALE = 1.0 / math.sqrt(HEAD_DIM + ROPE_DIM)

LANES = 128
BIAS_LANE = HEAD_DIM

PROJ_TM = 512
ATT_TQ = 512
ATT_TK = 256
OUT_TM = 512
FF_CHUNK = 1024

GROUP_W = N_HEADS * LANES

_C_FQ = 0
_C_FK = _C_FQ + GROUP_W
_C_LOGIT = _C_FK + GROUP_W
_C_QLAT = _C_LOGIT + LANES
_C_KVLAT = _C_QLAT + Q_LORA
_C_KROPE = _C_KVLAT + KV_LORA
_C_END = _C_KROPE + LANES


def _rms(x, g):
    return x * lax.rsqrt(jnp.mean(x * x, axis=-1, keepdims=True) + EPS) * g


def _split3(a):
    hi = a.astype(BF16)
    r = a - hi.astype(F32)
    mid = r.astype(BF16)
    lo = (r - mid.astype(F32)).astype(BF16)
    return jnp.concatenate([hi, mid, lo], axis=-1)


def _rope(x, cos, sin_a, sin_b):
    return x * cos + pltpu.roll(x, LANES - ROPE_DIM // 2, 1) * sin_a + pltpu.roll(x, ROPE_DIM // 2, 1) * sin_b


def _dot(a, b):
    return jnp.dot(a, b, preferred_element_type=F32)


def _dot_nt(a, b):
    return lax.dot_general(a, b, (((1,), (1,)), ((), ())), preferred_element_type=F32)


def _proj_kernel(x_ref, gmix_ref, win_ref, wvt_ref, bf_ref, gq_ref, wqup_ref, gkv_ref, wkup_ref,
                 wvupt_ref, cos_ref, sina_ref, sinb_ref, tri_ref, place_ref, qbias_ref,
                 q_out, k_out, vt_out, carry_ref):
    tm = x_ref.shape[0]

    @pl.when(pl.program_id(0) == 0)
    def _():
        carry_ref[...] = jnp.zeros_like(carry_ref)

    hb = _rms(x_ref[...], gmix_ref[...]).astype(BF16)

    pq = _dot(hb, win_ref[:, _C_FQ:_C_FQ + GROUP_W])
    q_out[:, 0:GROUP_W] = (pq * FOX_SCALE + qbias_ref[...]).astype(BF16)

    z = _dot(hb, win_ref[:, _C_LOGIT:_C_LOGIT + LANES]) + bf_ref[...]
    log_f = jnp.minimum(z, 0.0) - jnp.log1p(jnp.exp(-jnp.abs(z)))
    cs = _dot(tri_ref[...], _split3(log_f))
    c = cs[:, 0:LANES] + cs[:, LANES:2 * LANES] + cs[:, 2 * LANES:3 * LANES] + carry_ref[0:1, :]
    carry_ref[...] = jnp.broadcast_to(c[tm - 1:tm, :], carry_ref.shape)
    placed = _dot(_split3(-c), place_ref[...])
    pk = _dot(hb, win_ref[:, _C_FK:_C_FK + GROUP_W])
    k_out[:, 0:GROUP_W] = (pk + placed).astype(BF16)

    cos = cos_ref[...]
    sin_a = sina_ref[...]
    sin_b = sinb_ref[...]

    q_lat = _dot(hb, win_ref[:, _C_QLAT:_C_QLAT + Q_LORA])
    nq = _rms(q_lat, gq_ref[...]).astype(BF16)
    cq = _dot(nq, wqup_ref[...])
    cos_q, sina_q, sinb_q = cos * MLA_SCALE, sin_a * MLA_SCALE, sin_b * MLA_SCALE
    for h in range(N_HEADS):
        xg = cq[:, h * LANES:(h + 1) * LANES]
        q_out[:, GROUP_W + h * LANES:GROUP_W + (h + 1) * LANES] = _rope(xg, cos_q, sina_q, sinb_q).astype(BF16)

    kv_lat = _dot(hb, win_ref[:, _C_KVLAT:_C_KVLAT + KV_LORA])
    nkv = _rms(kv_lat, gkv_ref[...]).astype(BF16)
    kn = _dot(nkv, wkup_ref[...])
    kr = _rope(_dot(hb, win_ref[:, _C_KROPE:_C_KROPE + LANES]), cos, sin_a, sin_b)
    for h in range(N_HEADS):
        k_out[:, GROUP_W + h * LANES:GROUP_W + (h + 1) * LANES] = (kn[:, h * LANES:(h + 1) * LANES] + kr).astype(BF16)

    vf = _dot_nt(wvt_ref[...], hb).astype(BF16)
    vm = _dot_nt(wvupt_ref[...], nkv).astype(BF16)
    nv = vf.shape[0]
    for b in range(tm // ATT_TK):
        vt_out[b, 0:nv, :] = vf[:, b * ATT_TK:(b + 1) * ATT_TK]
        vt_out[b, nv:2 * nv, :] = vm[:, b * ATT_TK:(b + 1) * ATT_TK]


def _attn_kernel(q_ref, k_ref, vt_ref, o_ref, acc_ref, m_ref, l_ref):
    tq = q_ref.shape[0]
    tk = vt_ref.shape[2]
    pair = pl.program_id(0)
    i = pl.program_id(1)
    chunk_mask = jnp.where(pair >= N_HEADS // 2, CHUNK - 1, 0)
    n_full = i * (tq // tk)

    m_ref[...] = jnp.full_like(m_ref, -jnp.inf)
    l_ref[...] = jnp.zeros_like(l_ref)
    acc_ref[...] = jnp.zeros_like(acc_ref)

    def step(j, masked):
        for h in range(2):
            qh = q_ref[:, h * LANES:(h + 1) * LANES]
            kh = k_ref[pl.ds(pl.multiple_of(j * tk, tk), tk), h * LANES:(h + 1) * LANES]
            st = _dot_nt(kh, qh)
            if masked:
                kpos = j * tk + lax.broadcasted_iota(jnp.int32, (tk, tq), 0)
                qpos = i * tq + lax.broadcasted_iota(jnp.int32, (tk, tq), 1)
                st = jnp.where(kpos <= (qpos | chunk_mask), st, -jnp.inf)
            m_prev = m_ref[h]
            m_new = jnp.maximum(m_prev, jnp.max(st, axis=0, keepdims=True))
            alpha = jnp.exp(m_prev - m_new)
            pt = jnp.exp(st - m_new)
            l_ref[h] = alpha * l_ref[h] + jnp.sum(pt, axis=0, keepdims=True)
            acc_ref[h] = alpha * acc_ref[h] + _dot(vt_ref[j], pt.astype(BF16))
            m_ref[h] = m_new

    def full_step(j, carry):
        step(j, False)
        return carry

    lax.fori_loop(0, n_full, full_step, 0)
    for jj in range(tq // tk):
        step(n_full + jj, True)

    o0 = acc_ref[0] * (1.0 / l_ref[0])
    o1 = acc_ref[1] * (1.0 / l_ref[1])
    row = lax.broadcasted_iota(jnp.int32, o0.shape, 0)
    ot = jnp.where(row < HEAD_DIM, o0, o1)
    o_ref[...] = ot.T.astype(BF16)


def _out_kernel(o_ref, x_ref, wo_ref, gmlp_ref, w1_ref, w2_ref, gfin_ref, out_ref):
    x1 = x_ref[...] + _dot(o_ref[...], wo_ref[...])
    h2 = _rms(x1, gmlp_ref[...]).astype(BF16)
    y = x1
    for c in range(w1_ref.shape[1] // FF_CHUNK):
        u = _dot(h2, w1_ref[:, c * FF_CHUNK:(c + 1) * FF_CHUNK])
        a = jnp.square(jnp.maximum(u, 0.0)).astype(BF16)
        y = y + _dot(a, w2_ref[c * FF_CHUNK:(c + 1) * FF_CHUNK, :])
    out_ref[...] = _rms(y, gfin_ref[...])


def _pad_heads(w, head_w):
    k = w.shape[0]
    w3 = w.reshape(k, N_HEADS, head_w)
    return jnp.pad(w3, ((0, 0), (0, 0), (0, LANES - head_w))).reshape(k, GROUP_W)


def _const_tables(seq):
    half = ROPE_DIM // 2
    inv = ROPE_THETA ** (-jnp.arange(half, dtype=F32) / half)
    ang = jnp.arange(seq).astype(F32)[:, None] * inv[None, :]
    cos, sin = jnp.cos(ang), jnp.sin(ang)
    ones = jnp.ones((seq, HEAD_DIM), F32)
    zeros = jnp.zeros((seq, HEAD_DIM), F32)
    tail1 = jnp.ones((seq, LANES - HEAD_DIM - ROPE_DIM), F32)
    tail0 = jnp.zeros((seq, LANES - HEAD_DIM - ROPE_DIM), F32)
    z16 = jnp.zeros((seq, half), F32)
    cos_t = jnp.concatenate([ones, cos, cos, tail1], axis=1)
    sin_a = jnp.concatenate([zeros, -sin, z16, tail0], axis=1)
    sin_b = jnp.concatenate([zeros, z16, sin, tail0], axis=1)
    return cos_t, sin_a, sin_b


def _placement():
    e = np.zeros((3 * LANES, GROUP_W), np.float32)
    for t in range(3):
        for h in range(N_HEADS):
            e[t * LANES + h, h * LANES + BIAS_LANE + t] = 1.0
    qb = np.zeros((1, GROUP_W), np.float32)
    for h in range(N_HEADS):
        qb[0, h * LANES + BIAS_LANE:h * LANES + BIAS_LANE + 3] = 1.0
    return jnp.asarray(e, BF16), jnp.asarray(qb, F32)


def _resident(shape):
    nd = len(shape)
    return pl.BlockSpec(shape, lambda *_: (0,) * nd, pipeline_mode=pl.Buffered(1))


def _layer(x2, g_mix, w_in, b_f, g_q, w_q_up, g_kv, w_kv_up, w_o, g_mlp, w_ff1, w_ff2, g_out):
    seq, d = x2.shape
    fw = N_HEADS * HEAD_DIM
    splits = np.cumsum([fw, fw, fw, N_HEADS, Q_LORA, KV_LORA])
    w_fq, w_fk, w_fv, w_lg, w_ql, w_kvl, w_kr = jnp.split(w_in, splits, axis=1)
    w_kr_g = jnp.pad(w_kr, ((0, 0), (HEAD_DIM, LANES - HEAD_DIM - ROPE_DIM)))
    win = jnp.concatenate(
        [_pad_heads(w_fq, HEAD_DIM), _pad_heads(w_fk, HEAD_DIM),
         jnp.pad(w_lg, ((0, 0), (0, LANES - N_HEADS))), w_ql, w_kvl, w_kr_g], axis=1).astype(BF16)
    wvt = w_fv.T.astype(BF16)
    bf = jnp.pad(b_f, (0, LANES - N_HEADS)).reshape(1, LANES)
    wqup = _pad_heads(w_q_up, HEAD_DIM + ROPE_DIM).astype(BF16)
    wkv3 = w_kv_up.reshape(KV_LORA, N_HEADS, 2 * HEAD_DIM)
    wkup = _pad_heads(wkv3[:, :, :HEAD_DIM].reshape(KV_LORA, fw), HEAD_DIM).astype(BF16)
    wvupt = wkv3[:, :, HEAD_DIM:].reshape(KV_LORA, fw).T.astype(BF16)
    cos_t, sin_a, sin_b = _const_tables(seq)
    tri = jnp.asarray(np.tril(np.ones((PROJ_TM, PROJ_TM), np.float32)), BF16)
    place, qbias = _placement()

    n_kv = seq // ATT_TK
    row = lambda i: (i, 0)
    q_all, k_all, vt_all = pl.pallas_call(
        _proj_kernel,
        grid=(seq // PROJ_TM,),
        in_specs=[
            pl.BlockSpec((PROJ_TM, d), row),
            _resident((1, d)), _resident(win.shape), _resident(wvt.shape), _resident((1, LANES)),
            _resident((1, Q_LORA)), _resident(wqup.shape), _resident((1, KV_LORA)), _resident(wkup.shape),
            _resident(wvupt.shape),
            pl.BlockSpec((PROJ_TM, LANES), row), pl.BlockSpec((PROJ_TM, LANES), row),
            pl.BlockSpec((PROJ_TM, LANES), row),
            _resident(tri.shape), _resident(place.shape), _resident(qbias.shape),
        ],
        out_specs=[
            pl.BlockSpec((PROJ_TM, 2 * GROUP_W), row),
            pl.BlockSpec((PROJ_TM, 2 * GROUP_W), row),
            pl.BlockSpec((PROJ_TM // ATT_TK, 2 * fw, ATT_TK), lambda i: (i, 0, 0)),
        ],
        out_shape=[
            jax.ShapeDtypeStruct((seq, 2 * GROUP_W), BF16),
            jax.ShapeDtypeStruct((seq, 2 * GROUP_W), BF16),
            jax.ShapeDtypeStruct((n_kv, 2 * fw, ATT_TK), BF16),
        ],
        scratch_shapes=[pltpu.VMEM((8, LANES), F32)],
        compiler_params=pltpu.CompilerParams(
            dimension_semantics=("arbitrary",), vmem_limit_bytes=48 * 1024 * 1024),
        name="proj",
    )(x2, g_mix.reshape(1, d), win, wvt, bf, g_q.reshape(1, Q_LORA), wqup, g_kv.reshape(1, KV_LORA),
      wkup, wvupt, cos_t, sin_a, sin_b, tri, place, qbias)

    n_pairs = N_HEADS
    o_all = pl.pallas_call(
        _attn_kernel,
        grid=(n_pairs, seq // ATT_TQ),
        in_specs=[
            pl.BlockSpec((ATT_TQ, 2 * LANES), lambda p, i: (i, p)),
            pl.BlockSpec((seq, 2 * LANES), lambda p, i: (0, p)),
            pl.BlockSpec((n_kv, LANES, ATT_TK), lambda p, i: (0, p, 0)),
        ],
        out_specs=pl.BlockSpec((ATT_TQ, LANES), lambda p, i: (i, p)),
        out_shape=jax.ShapeDtypeStruct((seq, 2 * fw), BF16),
        scratch_shapes=[
            pltpu.VMEM((2, LANES, ATT_TQ), F32),
            pltpu.VMEM((2, 1, ATT_TQ), F32),
            pltpu.VMEM((2, 1, ATT_TQ), F32),
        ],
        compiler_params=pltpu.CompilerParams(
            dimension_semantics=("arbitrary", "arbitrary"), vmem_limit_bytes=48 * 1024 * 1024),
        name="attn",
    )(q_all, k_all, vt_all)

    d_ff = w_ff1.shape[1]
    return pl.pallas_call(
        _out_kernel,
        grid=(seq // OUT_TM,),
        in_specs=[
            pl.BlockSpec((OUT_TM, 2 * fw), row),
            pl.BlockSpec((OUT_TM, d), row),
            _resident((2 * fw, d)), _resident((1, d)), _resident((d, d_ff)), _resident((d_ff, d)),
            _resident((1, d)),
        ],
        out_specs=pl.BlockSpec((OUT_TM, d), row),
        out_shape=jax.ShapeDtypeStruct((seq, d), F32),
        compiler_params=pltpu.CompilerParams(
            dimension_semantics=("arbitrary",), vmem_limit_bytes=56 * 1024 * 1024),
        name="out_mlp",
    )(o_all, x2, w_o.astype(BF16), g_mlp.reshape(1, d), w_ff1.astype(BF16), w_ff2.astype(BF16),
      g_out.reshape(1, d))


def kernel(x, g_mix, w_in, b_f, g_q, w_q_up, g_kv, w_kv_up, w_o, g_mlp, w_ff1, w_ff2, g_final):
    b, seq, d = x.shape
    depth = w_in.shape[0]
    assert b == 1 and depth == 1, "single-sequence, single-layer problem"
    assert seq % PROJ_TM == 0 and seq % ATT_TQ == 0 and seq % OUT_TM == 0 and ATT_TQ % ATT_TK == 0
    out = _layer(x[0], g_mix[0], w_in[0], b_f[0], g_q[0], w_q_up[0], g_kv[0], w_kv_up[0], w_o[0],
                 g_mlp[0], w_ff1[0], w_ff2[0], g_final)
    return out[None]
```

```python
import functools
import math

import numpy as np
import jax
import jax.numpy as jnp
from jax import lax
from jax.experimental import pallas as pl
from jax.experimental.pallas import tpu as pltpu

F32 = jnp.float32
BF16 = jnp.bfloat16

EPS = 1e-6
CHUNK = 64
N_HEADS = 8
HEAD_DIM = 64
ROPE_DIM = 32
Q_LORA = 256
KV_LORA = 128
ROPE_THETA = 10000.0
LOG2E = math.log2(math.e)
FOX_SCALE = LOG2E / math.sqrt(HEAD_DIM)
MLA_SCALE = LOG2E / math.sqrt(HEAD_DIM + ROPE_DIM)

LANES = 128
BIAS_LANE = HEAD_DIM

PROJ_TM = 512
ATT_TQ = 512
ATT_TK = 512
OUT_TM = 512
FF_CHUNK = 1024

GROUP_W = N_HEADS * LANES

_C_FQ = 0
_C_FK = _C_FQ + GROUP_W
_C_LOGIT = _C_FK + GROUP_W
_C_QLAT = _C_LOGIT + LANES
_C_KVLAT = _C_QLAT + Q_LORA
_C_KROPE = _C_KVLAT + KV_LORA
_C_END = _C_KROPE + LANES


def _rms(x, g):
    return x * lax.rsqrt(jnp.mean(x * x, axis=-1, keepdims=True) + EPS) * g


def _split3(a):
    hi = a.astype(BF16)
    r = a - hi.astype(F32)
    mid = r.astype(BF16)
    lo = (r - mid.astype(F32)).astype(BF16)
    return jnp.concatenate([hi, mid, lo], axis=-1)


def _rope(x, cos, sin_a, sin_b):
    return x * cos + pltpu.roll(x, LANES - ROPE_DIM // 2, 1) * sin_a + pltpu.roll(x, ROPE_DIM // 2, 1) * sin_b


def _dot(a, b):
    return jnp.dot(a, b, preferred_element_type=F32)


def _dot_nt(a, b):
    return lax.dot_general(a, b, (((1,), (1,)), ((), ())), preferred_element_type=F32)


def _proj_kernel(x_ref, gmix_ref, win_ref, wvt_ref, bf_ref, gq_ref, wqup_ref, gkv_ref, wkup_ref,
                 wvupt_ref, cos_ref, sina_ref, sinb_ref, tri_ref, place_ref, qbias_ref,
                 q_out, k_out, vt_out, carry_ref):
    tm = x_ref.shape[0]

    @pl.when(pl.program_id(0) == 0)
    def _():
        carry_ref[...] = jnp.zeros_like(carry_ref)

    hb = _rms(x_ref[...], gmix_ref[...]).astype(BF16)

    pq = _dot(hb, win_ref[:, _C_FQ:_C_FQ + GROUP_W])
    q_out[:, 0:GROUP_W] = (pq * FOX_SCALE + qbias_ref[...]).astype(BF16)

    z = _dot(hb, win_ref[:, _C_LOGIT:_C_LOGIT + LANES]) + bf_ref[...]
    log_f = jnp.minimum(z, 0.0) - jnp.log1p(jnp.exp(-jnp.abs(z)))
    cs = _dot(tri_ref[...], _split3(log_f))
    c = cs[:, 0:LANES] + cs[:, LANES:2 * LANES] + cs[:, 2 * LANES:3 * LANES] + carry_ref[0:1, :]
    carry_ref[...] = jnp.broadcast_to(c[tm - 1:tm, :], carry_ref.shape)
    placed = _dot(_split3(c * -LOG2E), place_ref[...])
    pk = _dot(hb, win_ref[:, _C_FK:_C_FK + GROUP_W])
    k_out[:, 0:GROUP_W] = (pk + placed).astype(BF16)

    cos = cos_ref[...]
    sin_a = sina_ref[...]
    sin_b = sinb_ref[...]

    q_lat = _dot(hb, win_ref[:, _C_QLAT:_C_QLAT + Q_LORA])
    nq = _rms(q_lat, gq_ref[...]).astype(BF16)
    cq = _dot(nq, wqup_ref[...])
    cos_q, sina_q, sinb_q = cos * MLA_SCALE, sin_a * MLA_SCALE, sin_b * MLA_SCALE
    for h in range(N_HEADS):
        xg = cq[:, h * LANES:(h + 1) * LANES]
        q_out[:, GROUP_W + h * LANES:GROUP_W + (h + 1) * LANES] = _rope(xg, cos_q, sina_q, sinb_q).astype(BF16)

    kv_lat = _dot(hb, win_ref[:, _C_KVLAT:_C_KVLAT + KV_LORA])
    nkv = _rms(kv_lat, gkv_ref[...]).astype(BF16)
    kn = _dot(nkv, wkup_ref[...])
    kr = _rope(_dot(hb, win_ref[:, _C_KROPE:_C_KROPE + LANES]), cos, sin_a, sin_b)
    for h in range(N_HEADS):
        k_out[:, GROUP_W + h * LANES:GROUP_W + (h + 1) * LANES] = (kn[:, h * LANES:(h + 1) * LANES] + kr).astype(BF16)

    vf = _dot_nt(wvt_ref[...], hb).astype(BF16)
    vm = _dot_nt(wvupt_ref[...], nkv).astype(BF16)
    nv = vf.shape[0]
    for b in range(tm // ATT_TK):
        vt_out[b, 0:nv, :] = vf[:, b * ATT_TK:(b + 1) * ATT_TK]
        vt_out[b, nv:2 * nv, :] = vm[:, b * ATT_TK:(b + 1) * ATT_TK]


def _attn_kernel(q_ref, k_ref, vt_ref, o_ref, s_ref, acc_ref, m_ref, l_ref):
    tq = q_ref.shape[0]
    tk = vt_ref.shape[2]
    assert tq == tk
    pair = pl.program_id(0)
    i = pl.program_id(1)
    chunk_mask = jnp.where(pair >= N_HEADS // 2, CHUNK - 1, 0)

    m_ref[...] = jnp.full_like(m_ref, -jnp.inf)
    l_ref[...] = jnp.zeros_like(l_ref)
    acc_ref[...] = jnp.zeros_like(acc_ref)

    def scores(t, slot):
        for h in range(2):
            kh = k_ref[pl.ds(pl.multiple_of(t * tk, tk), tk), h * LANES:(h + 1) * LANES]
            s_ref[slot, h] = _dot_nt(kh, q_ref[:, h * LANES:(h + 1) * LANES])

    def softmax_pv(t, slot, masked):
        for h in range(2):
            st = s_ref[slot, h]
            if masked:
                kpos = t * tk + lax.broadcasted_iota(jnp.int32, (tk, tq), 0)
                qpos = i * tq + lax.broadcasted_iota(jnp.int32, (tk, tq), 1)
                st = jnp.where(kpos <= (qpos | chunk_mask), st, -jnp.inf)
            m_prev = m_ref[h]
            m_new = jnp.maximum(m_prev, jnp.max(st, axis=0, keepdims=True))
            alpha = jnp.exp2(m_prev - m_new)
            pt = jnp.exp2(st - m_new)
            l_ref[h] = alpha * l_ref[h] + jnp.sum(pt, axis=0, keepdims=True)
            acc_ref[h] = alpha * acc_ref[h] + _dot(vt_ref[t], pt.astype(BF16))
            m_ref[h] = m_new

    scores(0, 0)

    def two_tiles(jj, carry):
        t = 2 * jj
        scores(t + 1, 1)
        softmax_pv(t, 0, False)
        scores(t + 2, 0)
        softmax_pv(t + 1, 1, False)
        return carry

    lax.fori_loop(0, i // 2, two_tiles, 0)

    @pl.when(i % 2 == 1)
    def _():
        scores(i, 1)
        softmax_pv(i - 1, 0, False)
        softmax_pv(i, 1, True)

    @pl.when(i % 2 == 0)
    def _():
        softmax_pv(i, 0, True)

    o0 = acc_ref[0] * (1.0 / l_ref[0])
    o1 = acc_ref[1] * (1.0 / l_ref[1])
    row = lax.broadcasted_iota(jnp.int32, o0.shape, 0)
    ot = jnp.where(row < HEAD_DIM, o0, o1)
    o_ref[...] = ot.T.astype(BF16)


def _out_kernel(o_ref, x_ref, wo_ref, gmlp_ref, w1_ref, w2_ref, gfin_ref, out_ref):
    x1 = x_ref[...] + _dot(o_ref[...], wo_ref[...])
    h2 = _rms(x1, gmlp_ref[...]).astype(BF16)
    y = x1
    for c in range(w1_ref.shape[1] // FF_CHUNK):
        u = _dot(h2, w1_ref[:, c * FF_CHUNK:(c + 1) * FF_CHUNK])
        a = jnp.square(jnp.maximum(u, 0.0)).astype(BF16)
        y = y + _dot(a, w2_ref[c * FF_CHUNK:(c + 1) * FF_CHUNK, :])
    out_ref[...] = _rms(y, gfin_ref[...])


def _pad_heads(w, head_w):
    k = w.shape[0]
    w3 = w.reshape(k, N_HEADS, head_w)
    return jnp.pad(w3, ((0, 0), (0, 0), (0, LANES - head_w))).reshape(k, GROUP_W)


def _const_tables(seq):
    half = ROPE_DIM // 2
    inv = ROPE_THETA ** (-jnp.arange(half, dtype=F32) / half)
    ang = jnp.arange(seq).astype(F32)[:, None] * inv[None, :]
    cos, sin = jnp.cos(ang), jnp.sin(ang)
    ones = jnp.ones((seq, HEAD_DIM), F32)
    zeros = jnp.zeros((seq, HEAD_DIM), F32)
    tail1 = jnp.ones((seq, LANES - HEAD_DIM - ROPE_DIM), F32)
    tail0 = jnp.zeros((seq, LANES - HEAD_DIM - ROPE_DIM), F32)
    z16 = jnp.zeros((seq, half), F32)
    cos_t = jnp.concatenate([ones, cos, cos, tail1], axis=1)
    sin_a = jnp.concatenate([zeros, -sin, z16, tail0], axis=1)
    sin_b = jnp.concatenate([zeros, z16, sin, tail0], axis=1)
    return cos_t, sin_a, sin_b


def _placement():
    e = np.zeros((3 * LANES, GROUP_W), np.float32)
    for t in range(3):
        for h in range(N_HEADS):
            e[t * LANES + h, h * LANES + BIAS_LANE + t] = 1.0
    qb = np.zeros((1, GROUP_W), np.float32)
    for h in range(N_HEADS):
        qb[0, h * LANES + BIAS_LANE:h * LANES + BIAS_LANE + 3] = 1.0
    return jnp.asarray(e, BF16), jnp.asarray(qb, F32)


def _resident(shape):
    nd = len(shape)
    return pl.BlockSpec(shape, lambda *_: (0,) * nd, pipeline_mode=pl.Buffered(1))


def _layer(x2, g_mix, w_in, b_f, g_q, w_q_up, g_kv, w_kv_up, w_o, g_mlp, w_ff1, w_ff2, g_out):
    seq, d = x2.shape
    fw = N_HEADS * HEAD_DIM
    splits = np.cumsum([fw, fw, fw, N_HEADS, Q_LORA, KV_LORA])
    w_fq, w_fk, w_fv, w_lg, w_ql, w_kvl, w_kr = jnp.split(w_in, splits, axis=1)
    w_kr_g = jnp.pad(w_kr, ((0, 0), (HEAD_DIM, LANES - HEAD_DIM - ROPE_DIM)))
    win = jnp.concatenate(
        [_pad_heads(w_fq, HEAD_DIM), _pad_heads(w_fk, HEAD_DIM),
         jnp.pad(w_lg, ((0, 0), (0, LANES - N_HEADS))), w_ql, w_kvl, w_kr_g], axis=1).astype(BF16)
    wvt = w_fv.T.astype(BF16)
    bf = jnp.pad(b_f, (0, LANES - N_HEADS)).reshape(1, LANES)
    wqup = _pad_heads(w_q_up, HEAD_DIM + ROPE_DIM).astype(BF16)
    wkv3 = w_kv_up.reshape(KV_LORA, N_HEADS, 2 * HEAD_DIM)
    wkup = _pad_heads(wkv3[:, :, :HEAD_DIM].reshape(KV_LORA, fw), HEAD_DIM).astype(BF16)
    wvupt = wkv3[:, :, HEAD_DIM:].reshape(KV_LORA, fw).T.astype(BF16)
    cos_t, sin_a, sin_b = _const_tables(seq)
    tri = jnp.asarray(np.tril(np.ones((PROJ_TM, PROJ_TM), np.float32)), BF16)
    place, qbias = _placement()

    n_kv = seq // ATT_TK
    row = lambda i: (i, 0)
    q_all, k_all, vt_all = pl.pallas_call(
        _proj_kernel,
        grid=(seq // PROJ_TM,),
        in_specs=[
            pl.BlockSpec((PROJ_TM, d), row),
            _resident((1, d)), _resident(win.shape), _resident(wvt.shape), _resident((1, LANES)),
            _resident((1, Q_LORA)), _resident(wqup.shape), _resident((1, KV_LORA)), _resident(wkup.shape),
            _resident(wvupt.shape),
            pl.BlockSpec((PROJ_TM, LANES), row), pl.BlockSpec((PROJ_TM, LANES), row),
            pl.BlockSpec((PROJ_TM, LANES), row),
            _resident(tri.shape), _resident(place.shape), _resident(qbias.shape),
        ],
        out_specs=[
            pl.BlockSpec((PROJ_TM, 2 * GROUP_W), row),
            pl.BlockSpec((PROJ_TM, 2 * GROUP_W), row),
            pl.BlockSpec((PROJ_TM // ATT_TK, 2 * fw, ATT_TK), lambda i: (i, 0, 0)),
        ],
        out_shape=[
            jax.ShapeDtypeStruct((seq, 2 * GROUP_W), BF16),
            jax.ShapeDtypeStruct((seq, 2 * GROUP_W), BF16),
            jax.ShapeDtypeStruct((n_kv, 2 * fw, ATT_TK), BF16),
        ],
        scratch_shapes=[pltpu.VMEM((8, LANES), F32)],
        compiler_params=pltpu.CompilerParams(
            dimension_semantics=("arbitrary",), vmem_limit_bytes=48 * 1024 * 1024),
        name="proj",
    )(x2, g_mix.reshape(1, d), win, wvt, bf, g_q.reshape(1, Q_LORA), wqup, g_kv.reshape(1, KV_LORA),
      wkup, wvupt, cos_t, sin_a, sin_b, tri, place, qbias)

    n_pairs = N_HEADS
    o_all = pl.pallas_call(
        _attn_kernel,
        grid=(n_pairs, seq // ATT_TQ),
        in_specs=[
            pl.BlockSpec((ATT_TQ, 2 * LANES), lambda p, i: (i, p)),
            pl.BlockSpec((seq, 2 * LANES), lambda p, i: (0, p)),
            pl.BlockSpec((n_kv, LANES, ATT_TK), lambda p, i: (0, p, 0)),
        ],
        out_specs=pl.BlockSpec((ATT_TQ, LANES), lambda p, i: (i, p)),
        out_shape=jax.ShapeDtypeStruct((seq, 2 * fw), BF16),
        scratch_shapes=[
            pltpu.VMEM((2, 2, ATT_TK, ATT_TQ), F32),
            pltpu.VMEM((2, LANES, ATT_TQ), F32),
            pltpu.VMEM((2, 1, ATT_TQ), F32),
            pltpu.VMEM((2, 1, ATT_TQ), F32),
        ],
        compiler_params=pltpu.CompilerParams(
            dimension_semantics=("arbitrary", "arbitrary"), vmem_limit_bytes=48 * 1024 * 1024),
        name="attn",
    )(q_all, k_all, vt_all)

    d_ff = w_ff1.shape[1]
    return pl.pallas_call(
        _out_kernel,
        grid=(seq // OUT_TM,),
        in_specs=[
            pl.BlockSpec((OUT_TM, 2 * fw), row),
            pl.BlockSpec((OUT_TM, d), row),
            _resident((2 * fw, d)), _resident((1, d)), _resident((d, d_ff)), _resident((d_ff, d)),
            _resident((1, d)),
        ],
        out_specs=pl.BlockSpec((OUT_TM, d), row),
        out_shape=jax.ShapeDtypeStruct((seq, d), F32),
        compiler_params=pltpu.CompilerParams(
            dimension_semantics=("arbitrary",), vmem_limit_bytes=56 * 1024 * 1024),
        name="out_mlp",
    )(o_all, x2, w_o.astype(BF16), g_mlp.reshape(1, d), w_ff1.astype(BF16), w_ff2.astype(BF16),
      g_out.reshape(1, d))


def kernel(x, g_mix, w_in, b_f, g_q, w_q_up, g_kv, w_kv_up, w_o, g_mlp, w_ff1, w_ff2, g_final):
    b, seq, d = x.shape
    depth = w_in.shape[0]
    assert b == 1 and depth == 1, "single-sequence, single-layer problem"
    assert seq % PROJ_TM == 0 and seq % ATT_TQ == 0 and seq % OUT_TM == 0 and ATT_TQ % ATT_TK == 0
    out = _layer(x[0], g_mix[0], w_in[0], b_f[0], g_q[0], w_q_up[0], g_kv[0], w_kv_up[0], w_o[0],
                 g_mlp[0], w_ff1[0], w_ff2[0], g_final)
    return out[None]
```

```python
import functools
import math

import numpy as np
import jax
import jax.numpy as jnp
from jax import lax
from jax.experimental import pallas as pl
from jax.experimental.pallas import tpu as pltpu

F32 = jnp.float32
BF16 = jnp.bfloat16

EPS = 1e-6
CHUNK = 64
N_HEADS = 8
HEAD_DIM = 64
ROPE_DIM = 32
Q_LORA = 256
KV_LORA = 128
ROPE_THETA = 10000.0
LOG2E = math.log2(math.e)
FOX_SCALE = LOG2E / math.sqrt(HEAD_DIM)
MLA_SCALE = LOG2E / math.sqrt(HEAD_DIM + ROPE_DIM)

LANES = 128
BIAS_LANE = HEAD_DIM

PROJ_TM = 512
ATT_TQ = 512
ATT_TK = 512
OUT_TM = 512
FF_CHUNK = 1024

GROUP_W = N_HEADS * LANES

_C_FQ = 0
_C_FK = _C_FQ + GROUP_W
_C_LOGIT = _C_FK + GROUP_W
_C_QLAT = _C_LOGIT + LANES
_C_KVLAT = _C_QLAT + Q_LORA
_C_KROPE = _C_KVLAT + KV_LORA
_C_END = _C_KROPE + LANES


def _rms(x, g):
    return x * lax.rsqrt(jnp.mean(x * x, axis=-1, keepdims=True) + EPS) * g


def _split3(a):
    hi = a.astype(BF16)
    r = a - hi.astype(F32)
    mid = r.astype(BF16)
    lo = (r - mid.astype(F32)).astype(BF16)
    return jnp.concatenate([hi, mid, lo], axis=-1)


def _rope(x, cos, sin_a, sin_b):
    return x * cos + pltpu.roll(x, LANES - ROPE_DIM // 2, 1) * sin_a + pltpu.roll(x, ROPE_DIM // 2, 1) * sin_b


def _dot(a, b):
    return jnp.dot(a, b, preferred_element_type=F32)


def _dot_nt(a, b):
    return lax.dot_general(a, b, (((1,), (1,)), ((), ())), preferred_element_type=F32)


def _proj_kernel(x_ref, gmix_ref, win_ref, wvt_ref, bf_ref, gq_ref, wqup_ref, gkv_ref, wkup_ref,
                 wvupt_ref, cos_ref, sina_ref, sinb_ref, tri_ref, place_ref, qbias_ref,
                 q_out, k_out, vt_out, carry_ref):
    tm = x_ref.shape[0]

    @pl.when(pl.program_id(0) == 0)
    def _():
        carry_ref[...] = jnp.zeros_like(carry_ref)

    hb = _rms(x_ref[...], gmix_ref[...]).astype(BF16)

    pq = _dot(hb, win_ref[:, _C_FQ:_C_FQ + GROUP_W])
    q_out[:, 0:GROUP_W] = (pq * FOX_SCALE + qbias_ref[...]).astype(BF16)

    z = _dot(hb, win_ref[:, _C_LOGIT:_C_LOGIT + LANES]) + bf_ref[...]
    log_f = jnp.minimum(z, 0.0) - jnp.log1p(jnp.exp(-jnp.abs(z)))
    cs = _dot(tri_ref[...], _split3(log_f))
    c = cs[:, 0:LANES] + cs[:, LANES:2 * LANES] + cs[:, 2 * LANES:3 * LANES] + carry_ref[0:1, :]
    carry_ref[...] = jnp.broadcast_to(c[tm - 1:tm, :], carry_ref.shape)
    placed = _dot(_split3(c * -LOG2E), place_ref[...])
    pk = _dot(hb, win_ref[:, _C_FK:_C_FK + GROUP_W])
    k_out[:, 0:GROUP_W] = (pk + placed).astype(BF16)

    cos = cos_ref[...]
    sin_a = sina_ref[...]
    sin_b = sinb_ref[...]

    q_lat = _dot(hb, win_ref[:, _C_QLAT:_C_QLAT + Q_LORA])
    nq = _rms(q_lat, gq_ref[...]).astype(BF16)
    cq = _dot(nq, wqup_ref[...])
    cos_q, sina_q, sinb_q = cos * MLA_SCALE, sin_a * MLA_SCALE, sin_b * MLA_SCALE
    for h in range(N_HEADS):
        xg = cq[:, h * LANES:(h + 1) * LANES]
        q_out[:, GROUP_W + h * LANES:GROUP_W + (h + 1) * LANES] = _rope(xg, cos_q, sina_q, sinb_q).astype(BF16)

    kv_lat = _dot(hb, win_ref[:, _C_KVLAT:_C_KVLAT + KV_LORA])
    nkv = _rms(kv_lat, gkv_ref[...]).astype(BF16)
    kn = _dot(nkv, wkup_ref[...])
    kr = _rope(_dot(hb, win_ref[:, _C_KROPE:_C_KROPE + LANES]), cos, sin_a, sin_b)
    for h in range(N_HEADS):
        k_out[:, GROUP_W + h * LANES:GROUP_W + (h + 1) * LANES] = (kn[:, h * LANES:(h + 1) * LANES] + kr).astype(BF16)

    vf = _dot_nt(wvt_ref[...], hb).astype(BF16)
    vm = _dot_nt(wvupt_ref[...], nkv).astype(BF16)
    nv = vf.shape[0]
    for b in range(tm // ATT_TK):
        vt_out[b, 0:nv, :] = vf[:, b * ATT_TK:(b + 1) * ATT_TK]
        vt_out[b, nv:2 * nv, :] = vm[:, b * ATT_TK:(b + 1) * ATT_TK]


def _attn_kernel(q_ref, k_ref, vt_ref, o_ref, s_ref, mx_ref, acc_ref, m_ref, l_ref):
    tq = q_ref.shape[0]
    tk = vt_ref.shape[2]
    assert tq == tk
    pair = pl.program_id(0)
    i = pl.program_id(1)
    chunk_mask = jnp.where(pair >= N_HEADS // 2, CHUNK - 1, 0)

    m_ref[...] = jnp.full_like(m_ref, -jnp.inf)
    l_ref[...] = jnp.zeros_like(l_ref)
    acc_ref[...] = jnp.zeros_like(acc_ref)

    def scores(t, slot):
        for h in range(2):
            kh = k_ref[pl.ds(pl.multiple_of(t * tk, tk), tk), h * LANES:(h + 1) * LANES]
            st = _dot_nt(kh, q_ref[:, h * LANES:(h + 1) * LANES])
            s_ref[slot, h] = st
            mx_ref[slot, h] = jnp.max(st, axis=0, keepdims=True)

    def softmax_pv(t, slot, masked):
        for h in range(2):
            st = s_ref[slot, h]
            if masked:
                kpos = t * tk + lax.broadcasted_iota(jnp.int32, (tk, tq), 0)
                qpos = i * tq + lax.broadcasted_iota(jnp.int32, (tk, tq), 1)
                st = jnp.where(kpos <= (qpos | chunk_mask), st, -jnp.inf)
                tile_max = jnp.max(st, axis=0, keepdims=True)
            else:
                tile_max = mx_ref[slot, h]
            m_prev = m_ref[h]
            m_new = jnp.maximum(m_prev, tile_max)
            alpha = jnp.exp2(m_prev - m_new)
            pt = jnp.exp2(st - m_new)
            l_ref[h] = alpha * l_ref[h] + jnp.sum(pt, axis=0, keepdims=True)
            acc_ref[h] = alpha * acc_ref[h] + _dot(vt_ref[t], pt.astype(BF16))
            m_ref[h] = m_new

    scores(0, 0)

    def two_tiles(jj, carry):
        t = 2 * jj
        scores(t + 1, 1)
        softmax_pv(t, 0, False)
        scores(t + 2, 0)
        softmax_pv(t + 1, 1, False)
        return carry

    lax.fori_loop(0, i // 2, two_tiles, 0)

    @pl.when(i % 2 == 1)
    def _():
        scores(i, 1)
        softmax_pv(i - 1, 0, False)
        softmax_pv(i, 1, True)

    @pl.when(i % 2 == 0)
    def _():
        softmax_pv(i, 0, True)

    o0 = acc_ref[0] * (1.0 / l_ref[0])
    o1 = acc_ref[1] * (1.0 / l_ref[1])
    row = lax.broadcasted_iota(jnp.int32, o0.shape, 0)
    ot = jnp.where(row < HEAD_DIM, o0, o1)
    o_ref[...] = ot.T.astype(BF16)


def _out_kernel(o_ref, x_ref, wo_ref, gmlp_ref, w1_ref, w2_ref, gfin_ref, out_ref):
    x1 = x_ref[...] + _dot(o_ref[...], wo_ref[...])
    h2 = _rms(x1, gmlp_ref[...]).astype(BF16)
    y = x1
    for c in range(w1_ref.shape[1] // FF_CHUNK):
        u = _dot(h2, w1_ref[:, c * FF_CHUNK:(c + 1) * FF_CHUNK])
        a = jnp.square(jnp.maximum(u, 0.0)).astype(BF16)
        y = y + _dot(a, w2_ref[c * FF_CHUNK:(c + 1) * FF_CHUNK, :])
    out_ref[...] = _rms(y, gfin_ref[...])


def _pad_heads(w, head_w):
    k = w.shape[0]
    w3 = w.reshape(k, N_HEADS, head_w)
    return jnp.pad(w3, ((0, 0), (0, 0), (0, LANES - head_w))).reshape(k, GROUP_W)


def _const_tables(seq):
    half = ROPE_DIM // 2
    inv = ROPE_THETA ** (-jnp.arange(half, dtype=F32) / half)
    ang = jnp.arange(seq).astype(F32)[:, None] * inv[None, :]
    cos, sin = jnp.cos(ang), jnp.sin(ang)
    ones = jnp.ones((seq, HEAD_DIM), F32)
    zeros = jnp.zeros((seq, HEAD_DIM), F32)
    tail1 = jnp.ones((seq, LANES - HEAD_DIM - ROPE_DIM), F32)
    tail0 = jnp.zeros((seq, LANES - HEAD_DIM - ROPE_DIM), F32)
    z16 = jnp.zeros((seq, half), F32)
    cos_t = jnp.concatenate([ones, cos, cos, tail1], axis=1)
    sin_a = jnp.concatenate([zeros, -sin, z16, tail0], axis=1)
    sin_b = jnp.concatenate([zeros, z16, sin, tail0], axis=1)
    return cos_t, sin_a, sin_b


def _placement():
    e = np.zeros((3 * LANES, GROUP_W), np.float32)
    for t in range(3):
        for h in range(N_HEADS):
            e[t * LANES + h, h * LANES + BIAS_LANE + t] = 1.0
    qb = np.zeros((1, GROUP_W), np.float32)
    for h in range(N_HEADS):
        qb[0, h * LANES + BIAS_LANE:h * LANES + BIAS_LANE + 3] = 1.0
    return jnp.asarray(e, BF16), jnp.asarray(qb, F32)


def _resident(shape):
    nd = len(shape)
    return pl.BlockSpec(shape, lambda *_: (0,) * nd, pipeline_mode=pl.Buffered(1))


def _layer(x2, g_mix, w_in, b_f, g_q, w_q_up, g_kv, w_kv_up, w_o, g_mlp, w_ff1, w_ff2, g_out):
    seq, d = x2.shape
    fw = N_HEADS * HEAD_DIM
    splits = np.cumsum([fw, fw, fw, N_HEADS, Q_LORA, KV_LORA])
    w_fq, w_fk, w_fv, w_lg, w_ql, w_kvl, w_kr = jnp.split(w_in, splits, axis=1)
    w_kr_g = jnp.pad(w_kr, ((0, 0), (HEAD_DIM, LANES - HEAD_DIM - ROPE_DIM)))
    win = jnp.concatenate(
        [_pad_heads(w_fq, HEAD_DIM), _pad_heads(w_fk, HEAD_DIM),
         jnp.pad(w_lg, ((0, 0), (0, LANES - N_HEADS))), w_ql, w_kvl, w_kr_g], axis=1).astype(BF16)
    wvt = w_fv.T.astype(BF16)
    bf = jnp.pad(b_f, (0, LANES - N_HEADS)).reshape(1, LANES)
    wqup = _pad_heads(w_q_up, HEAD_DIM + ROPE_DIM).astype(BF16)
    wkv3 = w_kv_up.reshape(KV_LORA, N_HEADS, 2 * HEAD_DIM)
    wkup = _pad_heads(wkv3[:, :, :HEAD_DIM].reshape(KV_LORA, fw), HEAD_DIM).astype(BF16)
    wvupt = wkv3[:, :, HEAD_DIM:].reshape(KV_LORA, fw).T.astype(BF16)
    cos_t, sin_a, sin_b = _const_tables(seq)
    tri = jnp.asarray(np.tril(np.ones((PROJ_TM, PROJ_TM), np.float32)), BF16)
    place, qbias = _placement()

    n_kv = seq // ATT_TK
    row = lambda i: (i, 0)
    q_all, k_all, vt_all = pl.pallas_call(
        _proj_kernel,
        grid=(seq // PROJ_TM,),
        in_specs=[
            pl.BlockSpec((PROJ_TM, d), row),
            _resident((1, d)), _resident(win.shape), _resident(wvt.shape), _resident((1, LANES)),
            _resident((1, Q_LORA)), _resident(wqup.shape), _resident((1, KV_LORA)), _resident(wkup.shape),
            _resident(wvupt.shape),
            pl.BlockSpec((PROJ_TM, LANES), row), pl.BlockSpec((PROJ_TM, LANES), row),
            pl.BlockSpec((PROJ_TM, LANES), row),
            _resident(tri.shape), _resident(place.shape), _resident(qbias.shape),
        ],
        out_specs=[
            pl.BlockSpec((PROJ_TM, 2 * GROUP_W), row),
            pl.BlockSpec((PROJ_TM, 2 * GROUP_W), row),
            pl.BlockSpec((PROJ_TM // ATT_TK, 2 * fw, ATT_TK), lambda i: (i, 0, 0)),
        ],
        out_shape=[
            jax.ShapeDtypeStruct((seq, 2 * GROUP_W), BF16),
            jax.ShapeDtypeStruct((seq, 2 * GROUP_W), BF16),
            jax.ShapeDtypeStruct((n_kv, 2 * fw, ATT_TK), BF16),
        ],
        scratch_shapes=[pltpu.VMEM((8, LANES), F32)],
        compiler_params=pltpu.CompilerParams(
            dimension_semantics=("arbitrary",), vmem_limit_bytes=48 * 1024 * 1024),
        name="proj",
    )(x2, g_mix.reshape(1, d), win, wvt, bf, g_q.reshape(1, Q_LORA), wqup, g_kv.reshape(1, KV_LORA),
      wkup, wvupt, cos_t, sin_a, sin_b, tri, place, qbias)

    n_pairs = N_HEADS
    o_all = pl.pallas_call(
        _attn_kernel,
        grid=(n_pairs, seq // ATT_TQ),
        in_specs=[
            pl.BlockSpec((ATT_TQ, 2 * LANES), lambda p, i: (i, p)),
            pl.BlockSpec((seq, 2 * LANES), lambda p, i: (0, p)),
            pl.BlockSpec((n_kv, LANES, ATT_TK), lambda p, i: (0, p, 0)),
        ],
        out_specs=pl.BlockSpec((ATT_TQ, LANES), lambda p, i: (i, p)),
        out_shape=jax.ShapeDtypeStruct((seq, 2 * fw), BF16),
        scratch_shapes=[
            pltpu.VMEM((2, 2, ATT_TK, ATT_TQ), F32),
            pltpu.VMEM((2, 2, 1, ATT_TQ), F32),
            pltpu.VMEM((2, LANES, ATT_TQ), F32),
            pltpu.VMEM((2, 1, ATT_TQ), F32),
            pltpu.VMEM((2, 1, ATT_TQ), F32),
        ],
        compiler_params=pltpu.CompilerParams(
            dimension_semantics=("arbitrary", "arbitrary"), vmem_limit_bytes=48 * 1024 * 1024),
        name="attn",
    )(q_all, k_all, vt_all)

    d_ff = w_ff1.shape[1]
    return pl.pallas_call(
        _out_kernel,
        grid=(seq // OUT_TM,),
        in_specs=[
            pl.BlockSpec((OUT_TM, 2 * fw), row),
            pl.BlockSpec((OUT_TM, d), row),
            _resident((2 * fw, d)), _resident((1, d)), _resident((d, d_ff)), _resident((d_ff, d)),
            _resident((1, d)),
        ],
        out_specs=pl.BlockSpec((OUT_TM, d), row),
        out_shape=jax.ShapeDtypeStruct((seq, d), F32),
        compiler_params=pltpu.CompilerParams(
            dimension_semantics=("arbitrary",), vmem_limit_bytes=56 * 1024 * 1024),
        name="out_mlp",
    )(o_all, x2, w_o.astype(BF16), g_mlp.reshape(1, d), w_ff1.astype(BF16), w_ff2.astype(BF16),
      g_out.reshape(1, d))


def kernel(x, g_mix, w_in, b_f, g_q, w_q_up, g_kv, w_kv_up, w_o, g_mlp, w_ff1, w_ff2, g_final):
    b, seq, d = x.shape
    depth = w_in.shape[0]
    assert b == 1 and depth == 1, "single-sequence, single-layer problem"
    assert seq % PROJ_TM == 0 and seq % ATT_TQ == 0 and seq % OUT_TM == 0 and ATT_TQ % ATT_TK == 0
    out = _layer(x[0], g_mix[0], w_in[0], b_f[0], g_q[0], w_q_up[0], g_kv[0], w_kv_up[0], w_o[0],
                 g_mlp[0], w_ff1[0], w_ff2[0], g_final)
    return out[None]
```

```python
import math

import numpy as np
import jax
import jax.numpy as jnp
from jax import lax
from jax.experimental import pallas as pl
from jax.experimental.pallas import tpu as pltpu

F32 = jnp.float32
BF16 = jnp.bfloat16

EPS = 1e-6
CHUNK = 64
N_HEADS = 8
HEAD_DIM = 64
ROPE_DIM = 32
Q_LORA = 256
KV_LORA = 128
ROPE_THETA = 10000.0
LOG2E = math.log2(math.e)
FOX_SCALE = LOG2E / math.sqrt(HEAD_DIM)
MLA_SCALE = LOG2E / math.sqrt(HEAD_DIM + ROPE_DIM)

LANES = 128
BIAS_LANE = HEAD_DIM

PROJ_TM = 512
ATT_TQ = 512
ATT_TK = 512
OUT_TM = 512
FF_CHUNK = 1024

GROUP_W = N_HEADS * LANES

FOX_W = N_HEADS * HEAD_DIM
_C_FQ = 0
_C_FK = _C_FQ + FOX_W
_C_LOGIT = _C_FK + FOX_W
_C_QLAT = _C_LOGIT + LANES
_C_KVLAT = _C_QLAT + Q_LORA
_C_KROPE = _C_KVLAT + KV_LORA


def _rms(x, g):
    return x * lax.rsqrt(jnp.mean(x * x, axis=-1, keepdims=True) + EPS) * g


def _split3(a):
    hi = a.astype(BF16)
    r = a - hi.astype(F32)
    mid = r.astype(BF16)
    lo = (r - mid.astype(F32)).astype(BF16)
    return jnp.concatenate([hi, mid, lo], axis=-1)


def _rope(x, cos, sin, first_half):
    rot = jnp.where(first_half, pltpu.roll(x, LANES - ROPE_DIM // 2, 1), pltpu.roll(x, ROPE_DIM // 2, 1))
    return x * cos + rot * sin


def _spread_pair(pair, low, fill):
    return jnp.where(low, pair, fill), jnp.where(low, pltpu.roll(pair, HEAD_DIM, 1), fill)


def _dot(a, b):
    return jnp.dot(a, b, preferred_element_type=F32)


def _dot_nt(a, b):
    return lax.dot_general(a, b, (((1,), (1,)), ((), ())), preferred_element_type=F32)


def _proj_kernel(x_ref, gmix_ref, win_ref, wvt_ref, bf_ref, gq_ref, wqup_ref, gkv_ref, wkup_ref,
                 wvupt_ref, cos_ref, sin_ref, tri_ref, place_ref, qbias_ref,
                 q_out, k_out, vt_out, carry_ref):
    tm = x_ref.shape[0]

    @pl.when(pl.program_id(0) == 0)
    def _():
        carry_ref[...] = jnp.zeros_like(carry_ref)

    hb = _rms(x_ref[...], gmix_ref[...]).astype(BF16)

    lane = lax.broadcasted_iota(jnp.int32, (tm, LANES), 1)
    low = lane < HEAD_DIM

    pq = _dot(hb, win_ref[:, _C_FQ:_C_FQ + FOX_W]) * FOX_SCALE
    for j in range(N_HEADS // 2):
        even, odd = _spread_pair(pq[:, j * LANES:(j + 1) * LANES], low, qbias_ref[...])
        q_out[:, 2 * j * LANES:(2 * j + 1) * LANES] = even.astype(BF16)
        q_out[:, (2 * j + 1) * LANES:(2 * j + 2) * LANES] = odd.astype(BF16)

    z = _dot(hb, win_ref[:, _C_LOGIT:_C_LOGIT + LANES]) + bf_ref[...]
    log_f = jnp.minimum(z, 0.0) - jnp.log1p(jnp.exp(-jnp.abs(z)))
    cs = _dot(tri_ref[...], _split3(log_f))
    c = cs[:, 0:LANES] + cs[:, LANES:2 * LANES] + cs[:, 2 * LANES:3 * LANES] + carry_ref[0:1, :]
    carry_ref[...] = jnp.broadcast_to(c[tm - 1:tm, :], carry_ref.shape)
    placed = _dot(_split3(c * -LOG2E), place_ref[...])
    pk = _dot(hb, win_ref[:, _C_FK:_C_FK + FOX_W])
    for j in range(N_HEADS // 2):
        even, odd = 2 * j * LANES, (2 * j + 1) * LANES
        pair = pk[:, j * LANES:(j + 1) * LANES]
        k_out[:, even:even + LANES] = jnp.where(low, pair, placed[:, even:even + LANES]).astype(BF16)
        k_out[:, odd:odd + LANES] = jnp.where(
            low, pltpu.roll(pair, HEAD_DIM, 1), placed[:, odd:odd + LANES]).astype(BF16)

    cos = cos_ref[...]
    sin = sin_ref[...]
    first_half = lane < HEAD_DIM + ROPE_DIM // 2

    q_lat = _dot(hb, win_ref[:, _C_QLAT:_C_QLAT + Q_LORA])
    nq = _rms(q_lat, gq_ref[...]).astype(BF16)
    cq = _dot(nq, wqup_ref[...])
    cos_q, sin_q = cos * MLA_SCALE, sin * MLA_SCALE
    for h in range(N_HEADS):
        xg = cq[:, h * LANES:(h + 1) * LANES]
        q_out[:, GROUP_W + h * LANES:GROUP_W + (h + 1) * LANES] = _rope(xg, cos_q, sin_q, first_half).astype(BF16)

    kv_lat = _dot(hb, win_ref[:, _C_KVLAT:_C_KVLAT + KV_LORA])
    nkv = _rms(kv_lat, gkv_ref[...]).astype(BF16)
    kn = _dot(nkv, wkup_ref[...])
    kr = _rope(_dot(hb, win_ref[:, _C_KROPE:_C_KROPE + LANES]), cos, sin, first_half)
    for h in range(N_HEADS):
        k_out[:, GROUP_W + h * LANES:GROUP_W + (h + 1) * LANES] = (kn[:, h * LANES:(h + 1) * LANES] + kr).astype(BF16)

    vf = _dot_nt(wvt_ref[...], hb).astype(BF16)
    vm = _dot_nt(wvupt_ref[...], nkv).astype(BF16)
    nv = vf.shape[0]
    for b in range(tm // ATT_TK):
        vt_out[b, 0:nv, :] = vf[:, b * ATT_TK:(b + 1) * ATT_TK]
        vt_out[b, nv:2 * nv, :] = vm[:, b * ATT_TK:(b + 1) * ATT_TK]


def _attn_kernel(q_ref, k_ref, vt_ref, o_ref, s_ref, mx_ref, acc_ref, m_ref, l_ref):
    tq = q_ref.shape[0]
    tk = vt_ref.shape[2]
    assert tq == tk
    pair = pl.program_id(0)
    i = pl.program_id(1)
    chunk_mask = jnp.where(pair >= N_HEADS // 2, CHUNK - 1, 0)

    m_ref[...] = jnp.full_like(m_ref, -jnp.inf)
    l_ref[...] = jnp.zeros_like(l_ref)
    acc_ref[...] = jnp.zeros_like(acc_ref)

    def scores(t, slot):
        for h in range(2):
            kh = k_ref[pl.ds(pl.multiple_of(t * tk, tk), tk), h * LANES:(h + 1) * LANES]
            st = _dot_nt(kh, q_ref[:, h * LANES:(h + 1) * LANES])
            s_ref[slot, h] = st
            mx_ref[slot, h] = jnp.max(st, axis=0, keepdims=True)

    def softmax_pv(t, slot, masked):
        for h in range(2):
            st = s_ref[slot, h]
            if masked:
                kpos = t * tk + lax.broadcasted_iota(jnp.int32, (tk, tq), 0)
                qpos = i * tq + lax.broadcasted_iota(jnp.int32, (tk, tq), 1)
                st = jnp.where(kpos <= (qpos | chunk_mask), st, -jnp.inf)
                tile_max = jnp.max(st, axis=0, keepdims=True)
            else:
                tile_max = mx_ref[slot, h]
            m_prev = m_ref[h]
            m_new = jnp.maximum(m_prev, tile_max)
            alpha = jnp.exp2(m_prev - m_new)
            pt = jnp.exp2(st - m_new)
            l_ref[h] = alpha * l_ref[h] + jnp.sum(pt, axis=0, keepdims=True)
            acc_ref[h] = alpha * acc_ref[h] + _dot(vt_ref[t], pt.astype(BF16))
            m_ref[h] = m_new

    scores(0, 0)

    def full_tiles(t, n):
        for u in range(n):
            scores(t + u + 1, (u + 1) % 2)
            softmax_pv(t + u, u % 2, False)

    def four_tiles(jj, carry):
        full_tiles(4 * jj, 4)
        return carry

    lax.fori_loop(0, i // 4, four_tiles, 0)

    @pl.when(i % 4 >= 2)
    def _():
        full_tiles(4 * (i // 4), 2)

    @pl.when(i % 2 == 1)
    def _():
        scores(i, 1)
        softmax_pv(i - 1, 0, False)
        softmax_pv(i, 1, True)

    @pl.when(i % 2 == 0)
    def _():
        softmax_pv(i, 0, True)

    o0 = acc_ref[0] * (1.0 / l_ref[0])
    o1 = acc_ref[1] * (1.0 / l_ref[1])
    row = lax.broadcasted_iota(jnp.int32, o0.shape, 0)
    ot = jnp.where(row < HEAD_DIM, o0, o1)
    o_ref[...] = ot.T.astype(BF16)


def _out_kernel(o_ref, x_ref, wo_ref, gmlp_ref, w1_ref, w2_ref, gfin_ref, out_ref):
    x1 = x_ref[...] + _dot(o_ref[...], wo_ref[...])
    h2 = _rms(x1, gmlp_ref[...]).astype(BF16)
    y = x1
    for c in range(w1_ref.shape[1] // FF_CHUNK):
        u = _dot(h2, w1_ref[:, c * FF_CHUNK:(c + 1) * FF_CHUNK])
        a = jnp.square(jnp.maximum(u, 0.0)).astype(BF16)
        y = y + _dot(a, w2_ref[c * FF_CHUNK:(c + 1) * FF_CHUNK, :])
    out_ref[...] = _rms(y, gfin_ref[...])


def _pad_heads(w, head_w):
    k = w.shape[0]
    w3 = w.reshape(k, N_HEADS, head_w)
    return jnp.pad(w3, ((0, 0), (0, 0), (0, LANES - head_w))).reshape(k, GROUP_W)


def _rope_tables(seq):
    half = ROPE_DIM // 2
    inv = ROPE_THETA ** (-np.arange(half, dtype=np.float64) / half)
    ang = np.arange(seq, dtype=np.float64)[:, None] * inv[None, :]
    cos, sin = np.cos(ang), np.sin(ang)
    cos_t = np.ones((seq, LANES), np.float64)
    sin_t = np.zeros((seq, LANES), np.float64)
    cos_t[:, HEAD_DIM:HEAD_DIM + half] = cos
    cos_t[:, HEAD_DIM + half:HEAD_DIM + ROPE_DIM] = cos
    sin_t[:, HEAD_DIM:HEAD_DIM + half] = -sin
    sin_t[:, HEAD_DIM + half:HEAD_DIM + ROPE_DIM] = sin
    return jnp.asarray(cos_t, F32), jnp.asarray(sin_t, F32)


def _placement():
    e = np.zeros((3 * LANES, GROUP_W), np.float32)
    for t in range(3):
        for h in range(N_HEADS):
            e[t * LANES + h, h * LANES + BIAS_LANE + t] = 1.0
    qb = np.zeros((1, LANES), np.float32)
    qb[0, BIAS_LANE:BIAS_LANE + 3] = 1.0
    return jnp.asarray(e, BF16), jnp.asarray(qb, F32)


def _resident(shape):
    nd = len(shape)
    return pl.BlockSpec(shape, lambda *_: (0,) * nd, pipeline_mode=pl.Buffered(1))


def _layer(x2, g_mix, w_in, b_f, g_q, w_q_up, g_kv, w_kv_up, w_o, g_mlp, w_ff1, w_ff2, g_out):
    seq, d = x2.shape
    fw = FOX_W
    splits = np.cumsum([fw, fw, fw, N_HEADS, Q_LORA, KV_LORA])
    w_fq, w_fk, w_fv, w_lg, w_ql, w_kvl, w_kr = jnp.split(w_in, splits, axis=1)
    w_kr_g = jnp.pad(w_kr, ((0, 0), (HEAD_DIM, LANES - HEAD_DIM - ROPE_DIM)))
    win = jnp.concatenate(
        [w_fq, w_fk,
         jnp.pad(w_lg, ((0, 0), (0, LANES - N_HEADS))), w_ql, w_kvl, w_kr_g], axis=1).astype(BF16)
    wvt = w_fv.T.astype(BF16)
    bf = jnp.pad(b_f, (0, LANES - N_HEADS)).reshape(1, LANES)
    wqup = _pad_heads(w_q_up, HEAD_DIM + ROPE_DIM).astype(BF16)
    wkv3 = w_kv_up.reshape(KV_LORA, N_HEADS, 2 * HEAD_DIM)
    wkup = _pad_heads(wkv3[:, :, :HEAD_DIM].reshape(KV_LORA, fw), HEAD_DIM).astype(BF16)
    wvupt = wkv3[:, :, HEAD_DIM:].reshape(KV_LORA, fw).T.astype(BF16)
    cos_t, sin_t = _rope_tables(seq)
    tri = jnp.asarray(np.tril(np.ones((PROJ_TM, PROJ_TM), np.float32)), BF16)
    place, qbias = _placement()

    n_kv = seq // ATT_TK
    row = lambda i: (i, 0)
    q_all, k_all, vt_all = pl.pallas_call(
        _proj_kernel,
        grid=(seq // PROJ_TM,),
        in_specs=[
            pl.BlockSpec((PROJ_TM, d), row),
            _resident((1, d)), _resident(win.shape), _resident(wvt.shape), _resident((1, LANES)),
            _resident((1, Q_LORA)), _resident(wqup.shape), _resident((1, KV_LORA)), _resident(wkup.shape),
            _resident(wvupt.shape),
            pl.BlockSpec((PROJ_TM, LANES), row), pl.BlockSpec((PROJ_TM, LANES), row),
            _resident(tri.shape), _resident(place.shape), _resident(qbias.shape),
        ],
        out_specs=[
            pl.BlockSpec((PROJ_TM, 2 * GROUP_W), row),
            pl.BlockSpec((PROJ_TM, 2 * GROUP_W), row),
            pl.BlockSpec((PROJ_TM // ATT_TK, 2 * fw, ATT_TK), lambda i: (i, 0, 0)),
        ],
        out_shape=[
            jax.ShapeDtypeStruct((seq, 2 * GROUP_W), BF16),
            jax.ShapeDtypeStruct((seq, 2 * GROUP_W), BF16),
            jax.ShapeDtypeStruct((n_kv, 2 * fw, ATT_TK), BF16),
        ],
        scratch_shapes=[pltpu.VMEM((8, LANES), F32)],
        compiler_params=pltpu.CompilerParams(
            dimension_semantics=("arbitrary",), vmem_limit_bytes=48 * 1024 * 1024),
        name="proj",
    )(x2, g_mix.reshape(1, d), win, wvt, bf, g_q.reshape(1, Q_LORA), wqup, g_kv.reshape(1, KV_LORA),
      wkup, wvupt, cos_t, sin_t, tri, place, qbias)

    n_pairs = N_HEADS
    o_all = pl.pallas_call(
        _attn_kernel,
        grid=(n_pairs, seq // ATT_TQ),
        in_specs=[
            pl.BlockSpec((ATT_TQ, 2 * LANES), lambda p, i: (i, p)),
            pl.BlockSpec((seq, 2 * LANES), lambda p, i: (0, p)),
            pl.BlockSpec((n_kv, LANES, ATT_TK), lambda p, i: (0, p, 0)),
        ],
        out_specs=pl.BlockSpec((ATT_TQ, LANES), lambda p, i: (i, p)),
        out_shape=jax.ShapeDtypeStruct((seq, 2 * fw), BF16),
        scratch_shapes=[
            pltpu.VMEM((2, 2, ATT_TK, ATT_TQ), F32),
            pltpu.VMEM((2, 2, 1, ATT_TQ), F32),
            pltpu.VMEM((2, LANES, ATT_TQ), F32),
            pltpu.VMEM((2, 1, ATT_TQ), F32),
            pltpu.VMEM((2, 1, ATT_TQ), F32),
        ],
        compiler_params=pltpu.CompilerParams(
            dimension_semantics=("arbitrary", "arbitrary"), vmem_limit_bytes=48 * 1024 * 1024),
        name="attn",
    )(q_all, k_all, vt_all)

    d_ff = w_ff1.shape[1]
    return pl.pallas_call(
        _out_kernel,
        grid=(seq // OUT_TM,),
        in_specs=[
            pl.BlockSpec((OUT_TM, 2 * fw), row),
            pl.BlockSpec((OUT_TM, d), row),
            _resident((2 * fw, d)), _resident((1, d)), _resident((d, d_ff)), _resident((d_ff, d)),
            _resident((1, d)),
        ],
        out_specs=pl.BlockSpec((OUT_TM, d), row),
        out_shape=jax.ShapeDtypeStruct((seq, d), F32),
        compiler_params=pltpu.CompilerParams(
            dimension_semantics=("arbitrary",), vmem_limit_bytes=56 * 1024 * 1024),
        name="out_mlp",
    )(o_all, x2, w_o.astype(BF16), g_mlp.reshape(1, d), w_ff1.astype(BF16), w_ff2.astype(BF16),
      g_out.reshape(1, d))


def kernel(x, g_mix, w_in, b_f, g_q, w_q_up, g_kv, w_kv_up, w_o, g_mlp, w_ff1, w_ff2, g_final):
    b, seq, d = x.shape
    depth = w_in.shape[0]
    assert b == 1 and depth == 1, "single-sequence, single-layer problem"
    assert seq % PROJ_TM == 0 and seq % ATT_TQ == 0 and seq % OUT_TM == 0 and ATT_TQ % ATT_TK == 0
    out = _layer(x[0], g_mix[0], w_in[0], b_f[0], g_q[0], w_q_up[0], g_kv[0], w_kv_up[0], w_o[0],
                 g_mlp[0], w_ff1[0], w_ff2[0], g_final)
    return out[None]
```

```python
import math

import numpy as np
import jax
import jax.numpy as jnp
from jax import lax
from jax.experimental import pallas as pl
from jax.experimental.pallas import tpu as pltpu

F32 = jnp.float32
BF16 = jnp.bfloat16

EPS = 1e-6
CHUNK = 64
N_HEADS = 8
HEAD_DIM = 64
ROPE_DIM = 32
Q_LORA = 256
KV_LORA = 128
ROPE_THETA = 10000.0
LOG2E = math.log2(math.e)
FOX_SCALE = LOG2E / math.sqrt(HEAD_DIM)
MLA_SCALE = LOG2E / math.sqrt(HEAD_DIM + ROPE_DIM)

LANES = 128
BIAS_LANE = HEAD_DIM
MAX_JUMP = 60.0

PROJ_TM = 512
ATT_TQ = 512
ATT_TK = 512
OUT_TM = 512
FF_CHUNK = 1024

GROUP_W = N_HEADS * LANES

FOX_W = N_HEADS * HEAD_DIM
_C_FQ = 0
_C_FK = _C_FQ + FOX_W
_C_LOGIT = _C_FK + FOX_W
_C_QLAT = _C_LOGIT + LANES
_C_KVLAT = _C_QLAT + Q_LORA
_C_KROPE = _C_KVLAT + KV_LORA


def _rms(x, g):
    return x * lax.rsqrt(jnp.mean(x * x, axis=-1, keepdims=True) + EPS) * g


def _split3(a):
    hi = a.astype(BF16)
    r = a - hi.astype(F32)
    mid = r.astype(BF16)
    lo = (r - mid.astype(F32)).astype(BF16)
    return jnp.concatenate([hi, mid, lo], axis=-1)


def _rope(x, cos, sin, first_half):
    rot = jnp.where(first_half, pltpu.roll(x, LANES - ROPE_DIM // 2, 1), pltpu.roll(x, ROPE_DIM // 2, 1))
    return x * cos + rot * sin


def _spread_pair(pair, low, fill):
    return jnp.where(low, pair, fill), jnp.where(low, pltpu.roll(pair, HEAD_DIM, 1), fill)


def _dot(a, b):
    return jnp.dot(a, b, preferred_element_type=F32)


def _dot_nt(a, b):
    return lax.dot_general(a, b, (((1,), (1,)), ((), ())), preferred_element_type=F32)


def _proj_kernel(x_ref, gmix_ref, win_ref, wvt_ref, bf_ref, gq_ref, wqup_ref, gkv_ref, wkup_ref,
                 wvupt_ref, cos_ref, sin_ref, tri_ref, place_ref, qbias_ref,
                 q_out, k_out, vt_out, carry_ref):
    tm = x_ref.shape[0]

    @pl.when(pl.program_id(0) == 0)
    def _():
        carry_ref[...] = jnp.zeros_like(carry_ref)

    hb = _rms(x_ref[...], gmix_ref[...]).astype(BF16)

    lane = lax.broadcasted_iota(jnp.int32, (tm, LANES), 1)
    low = lane < HEAD_DIM

    pq = _dot(hb, win_ref[:, _C_FQ:_C_FQ + FOX_W]) * FOX_SCALE
    for j in range(N_HEADS // 2):
        even, odd = _spread_pair(pq[:, j * LANES:(j + 1) * LANES], low, qbias_ref[...])
        q_out[:, 2 * j * LANES:(2 * j + 1) * LANES] = even.astype(BF16)
        q_out[:, (2 * j + 1) * LANES:(2 * j + 2) * LANES] = odd.astype(BF16)

    z = _dot(hb, win_ref[:, _C_LOGIT:_C_LOGIT + LANES]) + bf_ref[...]
    log_f = jnp.minimum(z, 0.0) - jnp.log1p(jnp.exp(-jnp.abs(z)))
    cs = _dot(tri_ref[...], _split3(log_f))
    c = cs[:, 0:LANES] + cs[:, LANES:2 * LANES] + cs[:, 2 * LANES:3 * LANES] + carry_ref[0:1, :]
    carry_ref[...] = jnp.broadcast_to(c[tm - 1:tm, :], carry_ref.shape)
    placed = _dot(_split3(c * -LOG2E), place_ref[...])
    pk = _dot(hb, win_ref[:, _C_FK:_C_FK + FOX_W])
    for j in range(N_HEADS // 2):
        even, odd = 2 * j * LANES, (2 * j + 1) * LANES
        pair = pk[:, j * LANES:(j + 1) * LANES]
        k_out[:, even:even + LANES] = jnp.where(low, pair, placed[:, even:even + LANES]).astype(BF16)
        k_out[:, odd:odd + LANES] = jnp.where(
            low, pltpu.roll(pair, HEAD_DIM, 1), placed[:, odd:odd + LANES]).astype(BF16)

    cos = cos_ref[...]
    sin = sin_ref[...]
    first_half = lane < HEAD_DIM + ROPE_DIM // 2

    q_lat = _dot(hb, win_ref[:, _C_QLAT:_C_QLAT + Q_LORA])
    nq = _rms(q_lat, gq_ref[...]).astype(BF16)
    cq = _dot(nq, wqup_ref[...])
    cos_q, sin_q = cos * MLA_SCALE, sin * MLA_SCALE
    for h in range(N_HEADS):
        xg = cq[:, h * LANES:(h + 1) * LANES]
        q_out[:, GROUP_W + h * LANES:GROUP_W + (h + 1) * LANES] = _rope(xg, cos_q, sin_q, first_half).astype(BF16)

    kv_lat = _dot(hb, win_ref[:, _C_KVLAT:_C_KVLAT + KV_LORA])
    nkv = _rms(kv_lat, gkv_ref[...]).astype(BF16)
    kn = _dot(nkv, wkup_ref[...])
    kr = _rope(_dot(hb, win_ref[:, _C_KROPE:_C_KROPE + LANES]), cos, sin, first_half)
    for h in range(N_HEADS):
        k_out[:, GROUP_W + h * LANES:GROUP_W + (h + 1) * LANES] = (kn[:, h * LANES:(h + 1) * LANES] + kr).astype(BF16)

    vf = _dot_nt(wvt_ref[...], hb).astype(BF16)
    vm = _dot_nt(wvupt_ref[...], nkv).astype(BF16)
    nv = vf.shape[0]
    for b in range(tm // ATT_TK):
        vt_out[b, 0:nv, :] = vf[:, b * ATT_TK:(b + 1) * ATT_TK]
        vt_out[b, nv:2 * nv, :] = vm[:, b * ATT_TK:(b + 1) * ATT_TK]


def _attn_kernel(q_ref, k_ref, vt_ref, o_ref, p_ref, cs_ref, beta_ref, acc_ref, m_ref, l_ref, jump_ref):
    tq = q_ref.shape[0]
    tk = vt_ref.shape[2]
    assert tq == tk
    pair = pl.program_id(0)
    i = pl.program_id(1)
    chunk_mask = jnp.where(pair >= N_HEADS // 2, CHUNK - 1, 0)

    def qk(t, h):
        kh = k_ref[pl.ds(pl.multiple_of(t * tk, tk), tk), h * LANES:(h + 1) * LANES]
        return _dot_nt(kh, q_ref[:, h * LANES:(h + 1) * LANES])

    def diagonal_scores(h):
        kpos = i * tk + lax.broadcasted_iota(jnp.int32, (tk, tq), 0)
        qpos = i * tq + lax.broadcasted_iota(jnp.int32, (tk, tq), 1)
        return jnp.where(kpos <= (qpos | chunk_mask), qk(i, h), -jnp.inf)

    for h in range(2):
        st = diagonal_scores(h)
        m0 = jnp.max(st, axis=0, keepdims=True)
        pt = jnp.exp2(st - m0)
        m_ref[h] = m0
        l_ref[h] = jnp.sum(pt, axis=0, keepdims=True)
        acc_ref[h] = _dot(vt_ref[i], pt.astype(BF16))
        jump_ref[h] = jnp.zeros_like(m0)

    def stage_a(u, slot):
        for h in range(2):
            st = qk(i - 1 - u, h)
            m_old = m_ref[h]
            pt = jnp.exp2(st - m_old)
            p_ref[slot, h] = pt.astype(BF16)
            cs_ref[slot, h] = jnp.sum(pt, axis=0, keepdims=True)
            tile_max = jnp.max(st, axis=0, keepdims=True)
            m_new = jnp.maximum(m_old, tile_max)
            beta_ref[slot, h] = jnp.exp2(m_old - m_new)
            jump_ref[h] = jnp.maximum(jump_ref[h], tile_max - m_old)
            m_ref[h] = m_new

    def stage_u(u, slot):
        for h in range(2):
            beta = beta_ref[slot, h]
            acc_ref[h] = (acc_ref[h] + _dot(vt_ref[i - 1 - u], p_ref[slot, h])) * beta
            l_ref[h] = (l_ref[h] + cs_ref[slot, h]) * beta

    def full_tiles(u0, n):
        for k in range(n):
            stage_a(u0 + k + 1, (k + 1) % 2)
            stage_u(u0 + k, k % 2)

    n_ahead = jnp.maximum(i - 1, 0)

    @pl.when(i > 0)
    def _():
        stage_a(0, 0)

    def four_tiles(jj, carry):
        full_tiles(4 * jj, 4)
        return carry

    lax.fori_loop(0, n_ahead // 4, four_tiles, 0)

    @pl.when(n_ahead % 4 >= 2)
    def _():
        full_tiles(4 * (n_ahead // 4), 2)

    @pl.when(n_ahead % 2 == 1)
    def _():
        full_tiles(n_ahead - 1, 1)

    @pl.when(jnp.logical_and(i > 0, n_ahead % 2 == 0))
    def _():
        stage_u(n_ahead, 0)

    @pl.when(jnp.logical_and(i > 0, n_ahead % 2 == 1))
    def _():
        stage_u(n_ahead, 1)

    @pl.when(jnp.max(jump_ref[...]) > MAX_JUMP)
    def _():
        m_ref[...] = jnp.full_like(m_ref, -jnp.inf)
        l_ref[...] = jnp.zeros_like(l_ref)
        acc_ref[...] = jnp.zeros_like(acc_ref)

        def two_pass(t, st, h):
            m_prev = m_ref[h]
            m_new = jnp.maximum(m_prev, jnp.max(st, axis=0, keepdims=True))
            alpha = jnp.exp2(m_prev - m_new)
            pt = jnp.exp2(st - m_new)
            l_ref[h] = alpha * l_ref[h] + jnp.sum(pt, axis=0, keepdims=True)
            acc_ref[h] = alpha * acc_ref[h] + _dot(vt_ref[t], pt.astype(BF16))
            m_ref[h] = m_new

        def full_tile(t, carry):
            for h in range(2):
                two_pass(t, qk(t, h), h)
            return carry

        lax.fori_loop(0, i, full_tile, 0)
        for h in range(2):
            two_pass(i, diagonal_scores(h), h)

    o0 = acc_ref[0] * (1.0 / l_ref[0])
    o1 = acc_ref[1] * (1.0 / l_ref[1])
    row = lax.broadcasted_iota(jnp.int32, o0.shape, 0)
    ot = jnp.where(row < HEAD_DIM, o0, o1)
    o_ref[...] = ot.T.astype(BF16)


def _out_kernel(o_ref, x_ref, wo_ref, gmlp_ref, w1_ref, w2_ref, gfin_ref, out_ref):
    x1 = x_ref[...] + _dot(o_ref[...], wo_ref[...])
    h2 = _rms(x1, gmlp_ref[...]).astype(BF16)
    y = x1
    for c in range(w1_ref.shape[1] // FF_CHUNK):
        u = _dot(h2, w1_ref[:, c * FF_CHUNK:(c + 1) * FF_CHUNK])
        a = jnp.square(jnp.maximum(u, 0.0)).astype(BF16)
        y = y + _dot(a, w2_ref[c * FF_CHUNK:(c + 1) * FF_CHUNK, :])
    out_ref[...] = _rms(y, gfin_ref[...])


def _pad_heads(w, head_w):
    k = w.shape[0]
    w3 = w.reshape(k, N_HEADS, head_w)
    return jnp.pad(w3, ((0, 0), (0, 0), (0, LANES - head_w))).reshape(k, GROUP_W)


def _rope_tables(seq):
    half = ROPE_DIM // 2
    inv = ROPE_THETA ** (-np.arange(half, dtype=np.float64) / half)
    ang = np.arange(seq, dtype=np.float64)[:, None] * inv[None, :]
    cos, sin = np.cos(ang), np.sin(ang)
    cos_t = np.ones((seq, LANES), np.float64)
    sin_t = np.zeros((seq, LANES), np.float64)
    cos_t[:, HEAD_DIM:HEAD_DIM + half] = cos
    cos_t[:, HEAD_DIM + half:HEAD_DIM + ROPE_DIM] = cos
    sin_t[:, HEAD_DIM:HEAD_DIM + half] = -sin
    sin_t[:, HEAD_DIM + half:HEAD_DIM + ROPE_DIM] = sin
    return jnp.asarray(cos_t, F32), jnp.asarray(sin_t, F32)


def _placement():
    e = np.zeros((3 * LANES, GROUP_W), np.float32)
    for t in range(3):
        for h in range(N_HEADS):
            e[t * LANES + h, h * LANES + BIAS_LANE + t] = 1.0
    qb = np.zeros((1, LANES), np.float32)
    qb[0, BIAS_LANE:BIAS_LANE + 3] = 1.0
    return jnp.asarray(e, BF16), jnp.asarray(qb, F32)


def _resident(shape):
    nd = len(shape)
    return pl.BlockSpec(shape, lambda *_: (0,) * nd, pipeline_mode=pl.Buffered(1))


def _layer(x2, g_mix, w_in, b_f, g_q, w_q_up, g_kv, w_kv_up, w_o, g_mlp, w_ff1, w_ff2, g_out):
    seq, d = x2.shape
    fw = FOX_W
    splits = np.cumsum([fw, fw, fw, N_HEADS, Q_LORA, KV_LORA])
    w_fq, w_fk, w_fv, w_lg, w_ql, w_kvl, w_kr = jnp.split(w_in, splits, axis=1)
    w_kr_g = jnp.pad(w_kr, ((0, 0), (HEAD_DIM, LANES - HEAD_DIM - ROPE_DIM)))
    win = jnp.concatenate(
        [w_fq, w_fk,
         jnp.pad(w_lg, ((0, 0), (0, LANES - N_HEADS))), w_ql, w_kvl, w_kr_g], axis=1).astype(BF16)
    wvt = w_fv.T.astype(BF16)
    bf = jnp.pad(b_f, (0, LANES - N_HEADS)).reshape(1, LANES)
    wqup = _pad_heads(w_q_up, HEAD_DIM + ROPE_DIM).astype(BF16)
    wkv3 = w_kv_up.reshape(KV_LORA, N_HEADS, 2 * HEAD_DIM)
    wkup = _pad_heads(wkv3[:, :, :HEAD_DIM].reshape(KV_LORA, fw), HEAD_DIM).astype(BF16)
    wvupt = wkv3[:, :, HEAD_DIM:].reshape(KV_LORA, fw).T.astype(BF16)
    cos_t, sin_t = _rope_tables(seq)
    tri = jnp.asarray(np.tril(np.ones((PROJ_TM, PROJ_TM), np.float32)), BF16)
    place, qbias = _placement()

    n_kv = seq // ATT_TK
    row = lambda i: (i, 0)
    q_all, k_all, vt_all = pl.pallas_call(
        _proj_kernel,
        grid=(seq // PROJ_TM,),
        in_specs=[
            pl.BlockSpec((PROJ_TM, d), row),
            _resident((1, d)), _resident(win.shape), _resident(wvt.shape), _resident((1, LANES)),
            _resident((1, Q_LORA)), _resident(wqup.shape), _resident((1, KV_LORA)), _resident(wkup.shape),
            _resident(wvupt.shape),
            pl.BlockSpec((PROJ_TM, LANES), row), pl.BlockSpec((PROJ_TM, LANES), row),
            _resident(tri.shape), _resident(place.shape), _resident(qbias.shape),
        ],
        out_specs=[
            pl.BlockSpec((PROJ_TM, 2 * GROUP_W), row),
            pl.BlockSpec((PROJ_TM, 2 * GROUP_W), row),
            pl.BlockSpec((PROJ_TM // ATT_TK, 2 * fw, ATT_TK), lambda i: (i, 0, 0)),
        ],
        out_shape=[
            jax.ShapeDtypeStruct((seq, 2 * GROUP_W), BF16),
            jax.ShapeDtypeStruct((seq, 2 * GROUP_W), BF16),
            jax.ShapeDtypeStruct((n_kv, 2 * fw, ATT_TK), BF16),
        ],
        scratch_shapes=[pltpu.VMEM((8, LANES), F32)],
        compiler_params=pltpu.CompilerParams(
            dimension_semantics=("arbitrary",), vmem_limit_bytes=48 * 1024 * 1024),
        name="proj",
    )(x2, g_mix.reshape(1, d), win, wvt, bf, g_q.reshape(1, Q_LORA), wqup, g_kv.reshape(1, KV_LORA),
      wkup, wvupt, cos_t, sin_t, tri, place, qbias)

    n_pairs = N_HEADS
    o_all = pl.pallas_call(
        _attn_kernel,
        grid=(n_pairs, seq // ATT_TQ),
        in_specs=[
            pl.BlockSpec((ATT_TQ, 2 * LANES), lambda p, i: (i, p)),
            pl.BlockSpec((seq, 2 * LANES), lambda p, i: (0, p)),
            pl.BlockSpec((n_kv, LANES, ATT_TK), lambda p, i: (0, p, 0)),
        ],
        out_specs=pl.BlockSpec((ATT_TQ, LANES), lambda p, i: (i, p)),
        out_shape=jax.ShapeDtypeStruct((seq, 2 * fw), BF16),
        scratch_shapes=[
            pltpu.VMEM((2, 2, ATT_TK, ATT_TQ), BF16),
            pltpu.VMEM((2, 2, 1, ATT_TQ), F32),
            pltpu.VMEM((2, 2, 1, ATT_TQ), F32),
            pltpu.VMEM((2, LANES, ATT_TQ), F32),
            pltpu.VMEM((2, 1, ATT_TQ), F32),
            pltpu.VMEM((2, 1, ATT_TQ), F32),
            pltpu.VMEM((2, 1, ATT_TQ), F32),
        ],
        compiler_params=pltpu.CompilerParams(
            dimension_semantics=("arbitrary", "arbitrary"), vmem_limit_bytes=48 * 1024 * 1024),
        name="attn",
    )(q_all, k_all, vt_all)

    d_ff = w_ff1.shape[1]
    return pl.pallas_call(
        _out_kernel,
        grid=(seq // OUT_TM,),
        in_specs=[
            pl.BlockSpec((OUT_TM, 2 * fw), row),
            pl.BlockSpec((OUT_TM, d), row),
            _resident((2 * fw, d)), _resident((1, d)), _resident((d, d_ff)), _resident((d_ff, d)),
            _resident((1, d)),
        ],
        out_specs=pl.BlockSpec((OUT_TM, d), row),
        out_shape=jax.ShapeDtypeStruct((seq, d), F32),
        compiler_params=pltpu.CompilerParams(
            dimension_semantics=("arbitrary",), vmem_limit_bytes=56 * 1024 * 1024),
        name="out_mlp",
    )(o_all, x2, w_o.astype(BF16), g_mlp.reshape(1, d), w_ff1.astype(BF16), w_ff2.astype(BF16),
      g_out.reshape(1, d))


def kernel(x, g_mix, w_in, b_f, g_q, w_q_up, g_kv, w_kv_up, w_o, g_mlp, w_ff1, w_ff2, g_final):
    b, seq, d = x.shape
    depth = w_in.shape[0]
    assert b == 1 and depth == 1, "single-sequence, single-layer problem"
    assert seq % PROJ_TM == 0 and seq % ATT_TQ == 0 and seq % OUT_TM == 0 and ATT_TQ % ATT_TK == 0
    out = _layer(x[0], g_mix[0], w_in[0], b_f[0], g_q[0], w_q_up[0], g_kv[0], w_kv_up[0], w_o[0],
                 g_mlp[0], w_ff1[0], w_ff2[0], g_final)
    return out[None]
```

```python
import math

import numpy as np
import jax
import jax.numpy as jnp
from jax import lax
from jax.experimental import pallas as pl
from jax.experimental.pallas import tpu as pltpu

F32 = jnp.float32
BF16 = jnp.bfloat16

EPS = 1e-6
CHUNK = 64
N_HEADS = 8
HEAD_DIM = 64
ROPE_DIM = 32
Q_LORA = 256
KV_LORA = 128
ROPE_THETA = 10000.0
LOG2E = math.log2(math.e)
FOX_SCALE = LOG2E / math.sqrt(HEAD_DIM)
MLA_SCALE = LOG2E / math.sqrt(HEAD_DIM + ROPE_DIM)

LANES = 128
BIAS_LANE = HEAD_DIM
MAX_JUMP = 60.0

PROJ_TM = 512
ATT_TQ = 512
ATT_TK = 512
OUT_TM = 512
FF_CHUNK = 1024

GROUP_W = N_HEADS * LANES

FOX_W = N_HEADS * HEAD_DIM
_C_FQ = 0
_C_FK = _C_FQ + FOX_W
_C_LOGIT = _C_FK + FOX_W
_C_QLAT = _C_LOGIT + LANES
_C_KVLAT = _C_QLAT + Q_LORA
_C_KROPE = _C_KVLAT + KV_LORA


def _rms(x, g):
    return x * lax.rsqrt(jnp.mean(x * x, axis=-1, keepdims=True) + EPS) * g


def _split3(a):
    hi = a.astype(BF16)
    r = a - hi.astype(F32)
    mid = r.astype(BF16)
    lo = (r - mid.astype(F32)).astype(BF16)
    return jnp.concatenate([hi, mid, lo], axis=-1)


def _rope(x, cos, sin, first_half):
    rot = jnp.where(first_half, pltpu.roll(x, LANES - ROPE_DIM // 2, 1), pltpu.roll(x, ROPE_DIM // 2, 1))
    return x * cos + rot * sin


def _spread_pair(pair, low, fill):
    return jnp.where(low, pair, fill), jnp.where(low, pltpu.roll(pair, HEAD_DIM, 1), fill)


def _dot(a, b):
    return jnp.dot(a, b, preferred_element_type=F32)


def _dot_nt(a, b):
    return lax.dot_general(a, b, (((1,), (1,)), ((), ())), preferred_element_type=F32)


def _proj_kernel(x_ref, gmix_ref, win_ref, wvt_ref, bf_ref, gq_ref, wqup_ref, gkv_ref, wkup_ref,
                 wvupt_ref, cos_ref, sin_ref, tri_ref, place_ref, qbias_ref,
                 q_out, k_out, vt_out, carry_ref):
    tm = x_ref.shape[0]

    @pl.when(pl.program_id(0) == 0)
    def _():
        carry_ref[...] = jnp.zeros_like(carry_ref)

    hb = _rms(x_ref[...], gmix_ref[...]).astype(BF16)

    lane = lax.broadcasted_iota(jnp.int32, (tm, LANES), 1)
    low = lane < HEAD_DIM

    pq = _dot(hb, win_ref[:, _C_FQ:_C_FQ + FOX_W]) * FOX_SCALE
    for j in range(N_HEADS // 2):
        even, odd = _spread_pair(pq[:, j * LANES:(j + 1) * LANES], low, qbias_ref[...])
        q_out[:, 2 * j * LANES:(2 * j + 1) * LANES] = even.astype(BF16)
        q_out[:, (2 * j + 1) * LANES:(2 * j + 2) * LANES] = odd.astype(BF16)

    z = _dot(hb, win_ref[:, _C_LOGIT:_C_LOGIT + LANES]) + bf_ref[...]
    log_f = jnp.minimum(z, 0.0) - jnp.log1p(jnp.exp(-jnp.abs(z)))
    cs = _dot(tri_ref[...], _split3(log_f))
    c = cs[:, 0:LANES] + cs[:, LANES:2 * LANES] + cs[:, 2 * LANES:3 * LANES] + carry_ref[0:1, :]
    carry_ref[...] = jnp.broadcast_to(c[tm - 1:tm, :], carry_ref.shape)
    placed = _dot(_split3(c * -LOG2E), place_ref[...])
    pk = _dot(hb, win_ref[:, _C_FK:_C_FK + FOX_W])
    for j in range(N_HEADS // 2):
        even, odd = 2 * j * LANES, (2 * j + 1) * LANES
        pair = pk[:, j * LANES:(j + 1) * LANES]
        k_out[:, even:even + LANES] = jnp.where(low, pair, placed[:, even:even + LANES]).astype(BF16)
        k_out[:, odd:odd + LANES] = jnp.where(
            low, pltpu.roll(pair, HEAD_DIM, 1), placed[:, odd:odd + LANES]).astype(BF16)

    cos = cos_ref[...]
    sin = sin_ref[...]
    first_half = lane < HEAD_DIM + ROPE_DIM // 2

    q_lat = _dot(hb, win_ref[:, _C_QLAT:_C_QLAT + Q_LORA])
    nq = _rms(q_lat, gq_ref[...]).astype(BF16)
    cq = _dot(nq, wqup_ref[...])
    cos_q, sin_q = cos * MLA_SCALE, sin * MLA_SCALE
    for h in range(N_HEADS):
        xg = cq[:, h * LANES:(h + 1) * LANES]
        q_out[:, GROUP_W + h * LANES:GROUP_W + (h + 1) * LANES] = _rope(xg, cos_q, sin_q, first_half).astype(BF16)

    kv_lat = _dot(hb, win_ref[:, _C_KVLAT:_C_KVLAT + KV_LORA])
    nkv = _rms(kv_lat, gkv_ref[...]).astype(BF16)
    kn = _dot(nkv, wkup_ref[...])
    kr = _rope(_dot(hb, win_ref[:, _C_KROPE:_C_KROPE + LANES]), cos, sin, first_half)
    for h in range(N_HEADS):
        k_out[:, GROUP_W + h * LANES:GROUP_W + (h + 1) * LANES] = (kn[:, h * LANES:(h + 1) * LANES] + kr).astype(BF16)

    vf = _dot_nt(wvt_ref[...], hb).astype(BF16)
    vm = _dot_nt(wvupt_ref[...], nkv).astype(BF16)
    nv = vf.shape[0]
    for b in range(tm // ATT_TK):
        vt_out[b, 0:nv, :] = vf[:, b * ATT_TK:(b + 1) * ATT_TK]
        vt_out[b, nv:2 * nv, :] = vm[:, b * ATT_TK:(b + 1) * ATT_TK]


def _attn_kernel(q_ref, k_ref, vt_ref, o_ref, p_ref, cs_ref, beta_ref, acc_ref, m_ref, l_ref, jump_ref):
    tq = q_ref.shape[0]
    tk = vt_ref.shape[2]
    assert tq == tk
    pair = pl.program_id(0)
    i = pl.program_id(1)
    chunk_mask = jnp.where(pair >= N_HEADS // 2, CHUNK - 1, 0)

    def qk(t, h):
        kh = k_ref[pl.ds(pl.multiple_of(t * tk, tk), tk), h * LANES:(h + 1) * LANES]
        return _dot_nt(kh, q_ref[:, h * LANES:(h + 1) * LANES])

    def diagonal_scores(h):
        kpos = i * tk + lax.broadcasted_iota(jnp.int32, (tk, tq), 0)
        qpos = i * tq + lax.broadcasted_iota(jnp.int32, (tk, tq), 1)
        return jnp.where(kpos <= (qpos | chunk_mask), qk(i, h), -jnp.inf)

    l_ref[...] = jnp.zeros_like(l_ref)
    acc_ref[...] = jnp.zeros_like(acc_ref)

    def stage_a_diagonal():
        for h in range(2):
            st = diagonal_scores(h)
            m0 = jnp.max(st, axis=0, keepdims=True)
            pt = jnp.exp2(st - m0)
            p_ref[0, h] = pt.astype(BF16)
            cs_ref[0, h] = jnp.sum(pt, axis=0, keepdims=True)
            beta_ref[0, h] = jnp.ones_like(m0)
            jump_ref[h] = jnp.zeros_like(m0)
            m_ref[h] = m0

    def stage_a(u, slot):
        for h in range(2):
            st = qk(i - u, h)
            m_old = m_ref[h]
            pt = jnp.exp2(st - m_old)
            p_ref[slot, h] = pt.astype(BF16)
            cs_ref[slot, h] = jnp.sum(pt, axis=0, keepdims=True)
            tile_max = jnp.max(st, axis=0, keepdims=True)
            m_new = jnp.maximum(m_old, tile_max)
            beta_ref[slot, h] = jnp.exp2(m_old - m_new)
            jump_ref[h] = jnp.maximum(jump_ref[h], tile_max - m_old)
            m_ref[h] = m_new

    def stage_u(u, slot):
        for h in range(2):
            beta = beta_ref[slot, h]
            acc_ref[h] = (acc_ref[h] + _dot(vt_ref[i - u], p_ref[slot, h])) * beta
            l_ref[h] = (l_ref[h] + cs_ref[slot, h]) * beta

    def full_tiles(u0, n):
        for k in range(n):
            stage_a(u0 + k + 1, (k + 1) % 2)
            stage_u(u0 + k, k % 2)

    stage_a_diagonal()

    def four_tiles(jj, carry):
        full_tiles(4 * jj, 4)
        return carry

    lax.fori_loop(0, i // 4, four_tiles, 0)

    @pl.when(i % 4 >= 2)
    def _():
        full_tiles(4 * (i // 4), 2)

    @pl.when(i % 2 == 1)
    def _():
        full_tiles(i - 1, 1)
        stage_u(i, 1)

    @pl.when(i % 2 == 0)
    def _():
        stage_u(i, 0)

    @pl.when(jnp.max(jump_ref[...]) > MAX_JUMP)
    def _():
        m_ref[...] = jnp.full_like(m_ref, -jnp.inf)
        l_ref[...] = jnp.zeros_like(l_ref)
        acc_ref[...] = jnp.zeros_like(acc_ref)

        def two_pass(t, st, h):
            m_prev = m_ref[h]
            m_new = jnp.maximum(m_prev, jnp.max(st, axis=0, keepdims=True))
            alpha = jnp.exp2(m_prev - m_new)
            pt = jnp.exp2(st - m_new)
            l_ref[h] = alpha * l_ref[h] + jnp.sum(pt, axis=0, keepdims=True)
            acc_ref[h] = alpha * acc_ref[h] + _dot(vt_ref[t], pt.astype(BF16))
            m_ref[h] = m_new

        def full_tile(t, carry):
            for h in range(2):
                two_pass(t, qk(t, h), h)
            return carry

        lax.fori_loop(0, i, full_tile, 0)
        for h in range(2):
            two_pass(i, diagonal_scores(h), h)

    o0 = acc_ref[0] * (1.0 / l_ref[0])
    o1 = acc_ref[1] * (1.0 / l_ref[1])
    row = lax.broadcasted_iota(jnp.int32, o0.shape, 0)
    ot = jnp.where(row < HEAD_DIM, o0, o1)
    o_ref[...] = ot.T.astype(BF16)


def _out_kernel(o_ref, x_ref, wo_ref, gmlp_ref, w1_ref, w2_ref, gfin_ref, out_ref):
    x1 = x_ref[...] + _dot(o_ref[...], wo_ref[...])
    h2 = _rms(x1, gmlp_ref[...]).astype(BF16)
    y = x1
    for c in range(w1_ref.shape[1] // FF_CHUNK):
        u = _dot(h2, w1_ref[:, c * FF_CHUNK:(c + 1) * FF_CHUNK])
        a = jnp.square(jnp.maximum(u, 0.0)).astype(BF16)
        y = y + _dot(a, w2_ref[c * FF_CHUNK:(c + 1) * FF_CHUNK, :])
    out_ref[...] = _rms(y, gfin_ref[...])


def _pad_heads(w, head_w):
    k = w.shape[0]
    w3 = w.reshape(k, N_HEADS, head_w)
    return jnp.pad(w3, ((0, 0), (0, 0), (0, LANES - head_w))).reshape(k, GROUP_W)


def _rope_tables(seq):
    half = ROPE_DIM // 2
    inv = ROPE_THETA ** (-np.arange(half, dtype=np.float64) / half)
    ang = np.arange(seq, dtype=np.float64)[:, None] * inv[None, :]
    cos, sin = np.cos(ang), np.sin(ang)
    cos_t = np.ones((seq, LANES), np.float64)
    sin_t = np.zeros((seq, LANES), np.float64)
    cos_t[:, HEAD_DIM:HEAD_DIM + half] = cos
    cos_t[:, HEAD_DIM + half:HEAD_DIM + ROPE_DIM] = cos
    sin_t[:, HEAD_DIM:HEAD_DIM + half] = -sin
    sin_t[:, HEAD_DIM + half:HEAD_DIM + ROPE_DIM] = sin
    return jnp.asarray(cos_t, F32), jnp.asarray(sin_t, F32)


def _placement():
    e = np.zeros((3 * LANES, GROUP_W), np.float32)
    for t in range(3):
        for h in range(N_HEADS):
            e[t * LANES + h, h * LANES + BIAS_LANE + t] = 1.0
    qb = np.zeros((1, LANES), np.float32)
    qb[0, BIAS_LANE:BIAS_LANE + 3] = 1.0
    return jnp.asarray(e, BF16), jnp.asarray(qb, F32)


def _resident(shape):
    nd = len(shape)
    return pl.BlockSpec(shape, lambda *_: (0,) * nd, pipeline_mode=pl.Buffered(1))


def _layer(x2, g_mix, w_in, b_f, g_q, w_q_up, g_kv, w_kv_up, w_o, g_mlp, w_ff1, w_ff2, g_out):
    seq, d = x2.shape
    fw = FOX_W
    splits = np.cumsum([fw, fw, fw, N_HEADS, Q_LORA, KV_LORA])
    w_fq, w_fk, w_fv, w_lg, w_ql, w_kvl, w_kr = jnp.split(w_in, splits, axis=1)
    w_kr_g = jnp.pad(w_kr, ((0, 0), (HEAD_DIM, LANES - HEAD_DIM - ROPE_DIM)))
    win = jnp.concatenate(
        [w_fq, w_fk,
         jnp.pad(w_lg, ((0, 0), (0, LANES - N_HEADS))), w_ql, w_kvl, w_kr_g], axis=1).astype(BF16)
    wvt = w_fv.T.astype(BF16)
    bf = jnp.pad(b_f, (0, LANES - N_HEADS)).reshape(1, LANES)
    wqup = _pad_heads(w_q_up, HEAD_DIM + ROPE_DIM).astype(BF16)
    wkv3 = w_kv_up.reshape(KV_LORA, N_HEADS, 2 * HEAD_DIM)
    wkup = _pad_heads(wkv3[:, :, :HEAD_DIM].reshape(KV_LORA, fw), HEAD_DIM).astype(BF16)
    wvupt = wkv3[:, :, HEAD_DIM:].reshape(KV_LORA, fw).T.astype(BF16)
    cos_t, sin_t = _rope_tables(seq)
    tri = jnp.asarray(np.tril(np.ones((PROJ_TM, PROJ_TM), np.float32)), BF16)
    place, qbias = _placement()

    n_kv = seq // ATT_TK
    row = lambda i: (i, 0)
    q_all, k_all, vt_all = pl.pallas_call(
        _proj_kernel,
        grid=(seq // PROJ_TM,),
        in_specs=[
            pl.BlockSpec((PROJ_TM, d), row),
            _resident((1, d)), _resident(win.shape), _resident(wvt.shape), _resident((1, LANES)),
            _resident((1, Q_LORA)), _resident(wqup.shape), _resident((1, KV_LORA)), _resident(wkup.shape),
            _resident(wvupt.shape),
            pl.BlockSpec((PROJ_TM, LANES), row), pl.BlockSpec((PROJ_TM, LANES), row),
            _resident(tri.shape), _resident(place.shape), _resident(qbias.shape),
        ],
        out_specs=[
            pl.BlockSpec((PROJ_TM, 2 * GROUP_W), row),
            pl.BlockSpec((PROJ_TM, 2 * GROUP_W), row),
            pl.BlockSpec((PROJ_TM // ATT_TK, 2 * fw, ATT_TK), lambda i: (i, 0, 0)),
        ],
        out_shape=[
            jax.ShapeDtypeStruct((seq, 2 * GROUP_W), BF16),
            jax.ShapeDtypeStruct((seq, 2 * GROUP_W), BF16),
            jax.ShapeDtypeStruct((n_kv, 2 * fw, ATT_TK), BF16),
        ],
        scratch_shapes=[pltpu.VMEM((8, LANES), F32)],
        compiler_params=pltpu.CompilerParams(
            dimension_semantics=("arbitrary",), vmem_limit_bytes=48 * 1024 * 1024),
        name="proj",
    )(x2, g_mix.reshape(1, d), win, wvt, bf, g_q.reshape(1, Q_LORA), wqup, g_kv.reshape(1, KV_LORA),
      wkup, wvupt, cos_t, sin_t, tri, place, qbias)

    n_pairs = N_HEADS
    o_all = pl.pallas_call(
        _attn_kernel,
        grid=(n_pairs, seq // ATT_TQ),
        in_specs=[
            pl.BlockSpec((ATT_TQ, 2 * LANES), lambda p, i: (i, p)),
            pl.BlockSpec((seq, 2 * LANES), lambda p, i: (0, p)),
            pl.BlockSpec((n_kv, LANES, ATT_TK), lambda p, i: (0, p, 0)),
        ],
        out_specs=pl.BlockSpec((ATT_TQ, LANES), lambda p, i: (i, p)),
        out_shape=jax.ShapeDtypeStruct((seq, 2 * fw), BF16),
        scratch_shapes=[
            pltpu.VMEM((2, 2, ATT_TK, ATT_TQ), BF16),
            pltpu.VMEM((2, 2, 1, ATT_TQ), F32),
            pltpu.VMEM((2, 2, 1, ATT_TQ), F32),
            pltpu.VMEM((2, LANES, ATT_TQ), F32),
            pltpu.VMEM((2, 1, ATT_TQ), F32),
            pltpu.VMEM((2, 1, ATT_TQ), F32),
            pltpu.VMEM((2, 1, ATT_TQ), F32),
        ],
        compiler_params=pltpu.CompilerParams(
            dimension_semantics=("arbitrary", "arbitrary"), vmem_limit_bytes=48 * 1024 * 1024),
        name="attn",
    )(q_all, k_all, vt_all)

    d_ff = w_ff1.shape[1]
    return pl.pallas_call(
        _out_kernel,
        grid=(seq // OUT_TM,),
        in_specs=[
            pl.BlockSpec((OUT_TM, 2 * fw), row),
            pl.BlockSpec((OUT_TM, d), row),
            _resident((2 * fw, d)), _resident((1, d)), _resident((d, d_ff)), _resident((d_ff, d)),
            _resident((1, d)),
        ],
        out_specs=pl.BlockSpec((OUT_TM, d), row),
        out_shape=jax.ShapeDtypeStruct((seq, d), F32),
        compiler_params=pltpu.CompilerParams(
            dimension_semantics=("arbitrary",), vmem_limit_bytes=56 * 1024 * 1024),
        name="out_mlp",
    )(o_all, x2, w_o.astype(BF16), g_mlp.reshape(1, d), w_ff1.astype(BF16), w_ff2.astype(BF16),
      g_out.reshape(1, d))


def kernel(x, g_mix, w_in, b_f, g_q, w_q_up, g_kv, w_kv_up, w_o, g_mlp, w_ff1, w_ff2, g_final):
    b, seq, d = x.shape
    depth = w_in.shape[0]
    assert b == 1 and depth == 1, "single-sequence, single-layer problem"
    assert seq % PROJ_TM == 0 and seq % ATT_TQ == 0 and seq % OUT_TM == 0 and ATT_TQ % ATT_TK == 0
    out = _layer(x[0], g_mix[0], w_in[0], b_f[0], g_q[0], w_q_up[0], g_kv[0], w_kv_up[0], w_o[0],
                 g_mlp[0], w_ff1[0], w_ff2[0], g_final)
    return out[None]
```

```python
import math

import numpy as np
import jax
import jax.numpy as jnp
from jax import lax
from jax.experimental import pallas as pl
from jax.experimental.pallas import tpu as pltpu

F32 = jnp.float32
BF16 = jnp.bfloat16

EPS = 1e-6
CHUNK = 64
N_HEADS = 8
HEAD_DIM = 64
ROPE_DIM = 32
Q_LORA = 256
KV_LORA = 128
ROPE_THETA = 10000.0
LOG2E = math.log2(math.e)
FOX_SCALE = LOG2E / math.sqrt(HEAD_DIM)
MLA_SCALE = LOG2E / math.sqrt(HEAD_DIM + ROPE_DIM)

LANES = 128
BIAS_LANE = HEAD_DIM
MAX_JUMP = 60.0

PROJ_TM = 512
ATT_TQ = 512
ATT_TK = 512
OUT_TM = 512
FF_CHUNK = 1024

GROUP_W = N_HEADS * LANES

FOX_W = N_HEADS * HEAD_DIM
_C_FQ = 0
_C_FK = _C_FQ + FOX_W
_C_LOGIT = _C_FK + FOX_W
_C_QLAT = _C_LOGIT + LANES
_C_KVLAT = _C_QLAT + Q_LORA
_C_KROPE = _C_KVLAT + KV_LORA


def _rms(x, g):
    return x * lax.rsqrt(jnp.mean(x * x, axis=-1, keepdims=True) + EPS) * g


def _split3(a):
    hi = a.astype(BF16)
    r = a - hi.astype(F32)
    mid = r.astype(BF16)
    lo = (r - mid.astype(F32)).astype(BF16)
    return jnp.concatenate([hi, mid, lo], axis=-1)


def _rope(x, cos, sin, first_half):
    rot = jnp.where(first_half, pltpu.roll(x, LANES - ROPE_DIM // 2, 1), pltpu.roll(x, ROPE_DIM // 2, 1))
    return x * cos + rot * sin


def _spread_pair(pair, low, fill):
    return jnp.where(low, pair, fill), jnp.where(low, pltpu.roll(pair, HEAD_DIM, 1), fill)


def _dot(a, b):
    return jnp.dot(a, b, preferred_element_type=F32)


def _dot_nt(a, b):
    return lax.dot_general(a, b, (((1,), (1,)), ((), ())), preferred_element_type=F32)


def _proj_kernel(x_ref, gmix_ref, win_ref, wvt_ref, bf_ref, gq_ref, wqup_ref, gkv_ref, wkup_ref,
                 wvupt_ref, cos_ref, sin_ref, tri_ref, place_ref, qbias_ref,
                 q_out, k_out, vt_out, carry_ref):
    tm = x_ref.shape[0]

    @pl.when(pl.program_id(0) == 0)
    def _():
        carry_ref[...] = jnp.zeros_like(carry_ref)

    hb = _rms(x_ref[...], gmix_ref[...]).astype(BF16)

    lane = lax.broadcasted_iota(jnp.int32, (tm, LANES), 1)
    low = lane < HEAD_DIM

    pq = _dot(hb, win_ref[:, _C_FQ:_C_FQ + FOX_W]) * FOX_SCALE
    for j in range(N_HEADS // 2):
        even, odd = _spread_pair(pq[:, j * LANES:(j + 1) * LANES], low, qbias_ref[...])
        q_out[:, 2 * j * LANES:(2 * j + 1) * LANES] = even.astype(BF16)
        q_out[:, (2 * j + 1) * LANES:(2 * j + 2) * LANES] = odd.astype(BF16)

    z = _dot(hb, win_ref[:, _C_LOGIT:_C_LOGIT + LANES]) + bf_ref[...]
    log_f = jnp.minimum(z, 0.0) - jnp.log1p(jnp.exp(-jnp.abs(z)))
    cs = _dot(tri_ref[...], _split3(log_f))
    c = cs[:, 0:LANES] + cs[:, LANES:2 * LANES] + cs[:, 2 * LANES:3 * LANES] + carry_ref[0:1, :]
    carry_ref[...] = jnp.broadcast_to(c[tm - 1:tm, :], carry_ref.shape)
    placed = _dot(_split3(c * -LOG2E), place_ref[...])
    pk = _dot(hb, win_ref[:, _C_FK:_C_FK + FOX_W])
    for j in range(N_HEADS // 2):
        even, odd = 2 * j * LANES, (2 * j + 1) * LANES
        pair = pk[:, j * LANES:(j + 1) * LANES]
        k_out[:, even:even + LANES] = jnp.where(low, pair, placed[:, even:even + LANES]).astype(BF16)
        k_out[:, odd:odd + LANES] = jnp.where(
            low, pltpu.roll(pair, HEAD_DIM, 1), placed[:, odd:odd + LANES]).astype(BF16)

    cos = cos_ref[...]
    sin = sin_ref[...]
    first_half = lane < HEAD_DIM + ROPE_DIM // 2

    q_lat = _dot(hb, win_ref[:, _C_QLAT:_C_QLAT + Q_LORA])
    nq = _rms(q_lat, gq_ref[...]).astype(BF16)
    cq = _dot(nq, wqup_ref[...])
    cos_q, sin_q = cos * MLA_SCALE, sin * MLA_SCALE
    for h in range(N_HEADS):
        xg = cq[:, h * LANES:(h + 1) * LANES]
        q_out[:, GROUP_W + h * LANES:GROUP_W + (h + 1) * LANES] = _rope(xg, cos_q, sin_q, first_half).astype(BF16)

    kv_lat = _dot(hb, win_ref[:, _C_KVLAT:_C_KVLAT + KV_LORA])
    nkv = _rms(kv_lat, gkv_ref[...]).astype(BF16)
    kn = _dot(nkv, wkup_ref[...])
    kr = _rope(_dot(hb, win_ref[:, _C_KROPE:_C_KROPE + LANES]), cos, sin, first_half)
    for h in range(N_HEADS):
        k_out[:, GROUP_W + h * LANES:GROUP_W + (h + 1) * LANES] = (kn[:, h * LANES:(h + 1) * LANES] + kr).astype(BF16)

    vf = _dot_nt(wvt_ref[...], hb).astype(BF16)
    vm = _dot_nt(wvupt_ref[...], nkv).astype(BF16)
    nv = vf.shape[0]
    for b in range(tm // ATT_TK):
        vt_out[b, 0:nv, :] = vf[:, b * ATT_TK:(b + 1) * ATT_TK]
        vt_out[b, nv:2 * nv, :] = vm[:, b * ATT_TK:(b + 1) * ATT_TK]


def _attn_kernel(q_ref, k_ref, vt_ref, dmask_ref, o_ref, p_ref, cs_ref, beta_ref, acc_ref, m_ref, l_ref, jump_ref,
                 flag_ref):
    tq = q_ref.shape[0]
    tk = vt_ref.shape[2]
    assert tq == tk
    i = pl.program_id(1)

    def qk(t, h):
        kh = k_ref[pl.ds(pl.multiple_of(t * tk, tk), tk), h * LANES:(h + 1) * LANES]
        return _dot_nt(kh, q_ref[:, h * LANES:(h + 1) * LANES])

    def diagonal_scores(h):
        return qk(i, h) + dmask_ref[0]

    l_ref[...] = jnp.zeros_like(l_ref)
    acc_ref[...] = jnp.zeros_like(acc_ref)

    def stage_a_diagonal():
        for h in range(2):
            st = diagonal_scores(h)
            m0 = jnp.max(st, axis=0, keepdims=True)
            pt = jnp.exp2(st - m0)
            p_ref[0, h] = pt.astype(BF16)
            cs_ref[0, h] = jnp.sum(pt, axis=0, keepdims=True)
            beta_ref[0, h] = jnp.ones_like(m0)
            jump_ref[h] = jnp.zeros_like(m0)
            m_ref[h] = m0

    def stage_a(u, slot):
        for h in range(2):
            st = qk(i - u, h)
            m_old = m_ref[h]
            pt = jnp.exp2(st - m_old)
            p_ref[slot, h] = pt.astype(BF16)
            cs_ref[slot, h] = jnp.sum(pt, axis=0, keepdims=True)
            tile_max = jnp.max(st, axis=0, keepdims=True)
            m_new = jnp.maximum(m_old, tile_max)
            beta_ref[slot, h] = jnp.exp2(m_old - m_new)
            jump_ref[h] = jnp.maximum(jump_ref[h], tile_max - m_old)
            m_ref[h] = m_new

    def stage_u(u, slot):
        for h in range(2):
            beta = beta_ref[slot, h]
            acc_ref[h] = (acc_ref[h] + _dot(vt_ref[i - u], p_ref[slot, h])) * beta
            l_ref[h] = (l_ref[h] + cs_ref[slot, h]) * beta

    def full_tiles(u0, n):
        for k in range(n):
            stage_a(u0 + k + 1, (k + 1) % 2)
            stage_u(u0 + k, k % 2)

    stage_a_diagonal()

    def eight_tiles(jj, carry):
        full_tiles(8 * jj, 8)
        return carry

    lax.fori_loop(0, i // 8, eight_tiles, 0)

    @pl.when(i % 8 >= 4)
    def _():
        full_tiles(8 * (i // 8), 4)

    @pl.when(i % 4 >= 2)
    def _():
        full_tiles(4 * (i // 4), 2)

    def finish():
        o0 = acc_ref[0] * (1.0 / l_ref[0])
        o1 = acc_ref[1] * (1.0 / l_ref[1])
        row = lax.broadcasted_iota(jnp.int32, o0.shape, 0)
        ot = jnp.where(row < HEAD_DIM, o0, o1)
        o_ref[...] = ot.T.astype(BF16)

    @pl.when(i % 2 == 1)
    def _():
        full_tiles(i - 1, 1)
        flag_ref[0] = jnp.max(jump_ref[...])
        stage_u(i, 1)
        finish()

    @pl.when(i % 2 == 0)
    def _():
        flag_ref[0] = jnp.max(jump_ref[...])
        stage_u(i, 0)
        finish()

    @pl.when(flag_ref[0] > MAX_JUMP)
    def _():
        m_ref[...] = jnp.full_like(m_ref, -jnp.inf)
        l_ref[...] = jnp.zeros_like(l_ref)
        acc_ref[...] = jnp.zeros_like(acc_ref)

        def two_pass(t, st, h):
            m_prev = m_ref[h]
            m_new = jnp.maximum(m_prev, jnp.max(st, axis=0, keepdims=True))
            alpha = jnp.exp2(m_prev - m_new)
            pt = jnp.exp2(st - m_new)
            l_ref[h] = alpha * l_ref[h] + jnp.sum(pt, axis=0, keepdims=True)
            acc_ref[h] = alpha * acc_ref[h] + _dot(vt_ref[t], pt.astype(BF16))
            m_ref[h] = m_new

        def full_tile(t, carry):
            for h in range(2):
                two_pass(t, qk(t, h), h)
            return carry

        lax.fori_loop(0, i, full_tile, 0)
        for h in range(2):
            two_pass(i, diagonal_scores(h), h)
        finish()


def _out_kernel(o_ref, x_ref, wo_ref, gmlp_ref, w1_ref, w2_ref, gfin_ref, out_ref):
    x1 = x_ref[...] + _dot(o_ref[...], wo_ref[...])
    h2 = _rms(x1, gmlp_ref[...]).astype(BF16)
    y = x1
    for c in range(w1_ref.shape[1] // FF_CHUNK):
        u = _dot(h2, w1_ref[:, c * FF_CHUNK:(c + 1) * FF_CHUNK])
        a = jnp.square(jnp.maximum(u, 0.0)).astype(BF16)
        y = y + _dot(a, w2_ref[c * FF_CHUNK:(c + 1) * FF_CHUNK, :])
    out_ref[...] = _rms(y, gfin_ref[...])


def _pad_heads(w, head_w):
    k = w.shape[0]
    w3 = w.reshape(k, N_HEADS, head_w)
    return jnp.pad(w3, ((0, 0), (0, 0), (0, LANES - head_w))).reshape(k, GROUP_W)


def _rope_tables(seq):
    half = ROPE_DIM // 2
    inv = ROPE_THETA ** (-np.arange(half, dtype=np.float64) / half)
    ang = np.arange(seq, dtype=np.float64)[:, None] * inv[None, :]
    cos, sin = np.cos(ang), np.sin(ang)
    cos_t = np.ones((seq, LANES), np.float64)
    sin_t = np.zeros((seq, LANES), np.float64)
    cos_t[:, HEAD_DIM:HEAD_DIM + half] = cos
    cos_t[:, HEAD_DIM + half:HEAD_DIM + ROPE_DIM] = cos
    sin_t[:, HEAD_DIM:HEAD_DIM + half] = -sin
    sin_t[:, HEAD_DIM + half:HEAD_DIM + ROPE_DIM] = sin
    return jnp.asarray(cos_t, F32), jnp.asarray(sin_t, F32)


def _placement():
    e = np.zeros((3 * LANES, GROUP_W), np.float32)
    for t in range(3):
        for h in range(N_HEADS):
            e[t * LANES + h, h * LANES + BIAS_LANE + t] = 1.0
    qb = np.zeros((1, LANES), np.float32)
    qb[0, BIAS_LANE:BIAS_LANE + 3] = 1.0
    return jnp.asarray(e, BF16), jnp.asarray(qb, F32)


def _diagonal_masks():
    kpos = np.arange(ATT_TK)[:, None]
    qpos = np.arange(ATT_TQ)[None, :]
    visible = np.stack([kpos <= qpos, kpos // CHUNK <= qpos // CHUNK])
    return jnp.asarray(np.where(visible, 0.0, -np.inf), F32)


def _resident(shape):
    nd = len(shape)
    return pl.BlockSpec(shape, lambda *_: (0,) * nd, pipeline_mode=pl.Buffered(1))


def _layer(x2, g_mix, w_in, b_f, g_q, w_q_up, g_kv, w_kv_up, w_o, g_mlp, w_ff1, w_ff2, g_out):
    seq, d = x2.shape
    fw = FOX_W
    splits = np.cumsum([fw, fw, fw, N_HEADS, Q_LORA, KV_LORA])
    w_fq, w_fk, w_fv, w_lg, w_ql, w_kvl, w_kr = jnp.split(w_in, splits, axis=1)
    w_kr_g = jnp.pad(w_kr, ((0, 0), (HEAD_DIM, LANES - HEAD_DIM - ROPE_DIM)))
    win = jnp.concatenate(
        [w_fq, w_fk,
         jnp.pad(w_lg, ((0, 0), (0, LANES - N_HEADS))), w_ql, w_kvl, w_kr_g], axis=1).astype(BF16)
    wvt = w_fv.T.astype(BF16)
    bf = jnp.pad(b_f, (0, LANES - N_HEADS)).reshape(1, LANES)
    wqup = _pad_heads(w_q_up, HEAD_DIM + ROPE_DIM).astype(BF16)
    wkv3 = w_kv_up.reshape(KV_LORA, N_HEADS, 2 * HEAD_DIM)
    wkup = _pad_heads(wkv3[:, :, :HEAD_DIM].reshape(KV_LORA, fw), HEAD_DIM).astype(BF16)
    wvupt = wkv3[:, :, HEAD_DIM:].reshape(KV_LORA, fw).T.astype(BF16)
    cos_t, sin_t = _rope_tables(seq)
    tri = jnp.asarray(np.tril(np.ones((PROJ_TM, PROJ_TM), np.float32)), BF16)
    place, qbias = _placement()

    n_kv = seq // ATT_TK
    row = lambda i: (i, 0)
    q_all, k_all, vt_all = pl.pallas_call(
        _proj_kernel,
        grid=(seq // PROJ_TM,),
        in_specs=[
            pl.BlockSpec((PROJ_TM, d), row),
            _resident((1, d)), _resident(win.shape), _resident(wvt.shape), _resident((1, LANES)),
            _resident((1, Q_LORA)), _resident(wqup.shape), _resident((1, KV_LORA)), _resident(wkup.shape),
            _resident(wvupt.shape),
            pl.BlockSpec((PROJ_TM, LANES), row), pl.BlockSpec((PROJ_TM, LANES), row),
            _resident(tri.shape), _resident(place.shape), _resident(qbias.shape),
        ],
        out_specs=[
            pl.BlockSpec((PROJ_TM, 2 * GROUP_W), row),
            pl.BlockSpec((PROJ_TM, 2 * GROUP_W), row),
            pl.BlockSpec((PROJ_TM // ATT_TK, 2 * fw, ATT_TK), lambda i: (i, 0, 0)),
        ],
        out_shape=[
            jax.ShapeDtypeStruct((seq, 2 * GROUP_W), BF16),
            jax.ShapeDtypeStruct((seq, 2 * GROUP_W), BF16),
            jax.ShapeDtypeStruct((n_kv, 2 * fw, ATT_TK), BF16),
        ],
        scratch_shapes=[pltpu.VMEM((8, LANES), F32)],
        compiler_params=pltpu.CompilerParams(
            dimension_semantics=("arbitrary",), vmem_limit_bytes=48 * 1024 * 1024),
        name="proj",
    )(x2, g_mix.reshape(1, d), win, wvt, bf, g_q.reshape(1, Q_LORA), wqup, g_kv.reshape(1, KV_LORA),
      wkup, wvupt, cos_t, sin_t, tri, place, qbias)

    n_pairs = N_HEADS
    o_all = pl.pallas_call(
        _attn_kernel,
        grid=(n_pairs, seq // ATT_TQ),
        in_specs=[
            pl.BlockSpec((ATT_TQ, 2 * LANES), lambda p, i: (i, p)),
            pl.BlockSpec((seq, 2 * LANES), lambda p, i: (0, p)),
            pl.BlockSpec((n_kv, LANES, ATT_TK), lambda p, i: (0, p, 0)),
            pl.BlockSpec((1, ATT_TK, ATT_TQ), lambda p, i: (p // (N_HEADS // 2), 0, 0)),
        ],
        out_specs=pl.BlockSpec((ATT_TQ, LANES), lambda p, i: (i, p)),
        out_shape=jax.ShapeDtypeStruct((seq, 2 * fw), BF16),
        scratch_shapes=[
            pltpu.VMEM((2, 2, ATT_TK, ATT_TQ), BF16),
            pltpu.VMEM((2, 2, 1, ATT_TQ), F32),
            pltpu.VMEM((2, 2, 1, ATT_TQ), F32),
            pltpu.VMEM((2, LANES, ATT_TQ), F32),
            pltpu.VMEM((2, 1, ATT_TQ), F32),
            pltpu.VMEM((2, 1, ATT_TQ), F32),
            pltpu.VMEM((2, 1, ATT_TQ), F32),
            pltpu.SMEM((1,), F32),
        ],
        compiler_params=pltpu.CompilerParams(
            dimension_semantics=("arbitrary", "arbitrary"), vmem_limit_bytes=48 * 1024 * 1024),
        name="attn",
    )(q_all, k_all, vt_all, _diagonal_masks())

    d_ff = w_ff1.shape[1]
    return pl.pallas_call(
        _out_kernel,
        grid=(seq // OUT_TM,),
        in_specs=[
            pl.BlockSpec((OUT_TM, 2 * fw), row),
            pl.BlockSpec((OUT_TM, d), row),
            _resident((2 * fw, d)), _resident((1, d)), _resident((d, d_ff)), _resident((d_ff, d)),
            _resident((1, d)),
        ],
        out_specs=pl.BlockSpec((OUT_TM, d), row),
        out_shape=jax.ShapeDtypeStruct((seq, d), F32),
        compiler_params=pltpu.CompilerParams(
            dimension_semantics=("arbitrary",), vmem_limit_bytes=56 * 1024 * 1024),
        name="out_mlp",
    )(o_all, x2, w_o.astype(BF16), g_mlp.reshape(1, d), w_ff1.astype(BF16), w_ff2.astype(BF16),
      g_out.reshape(1, d))


def kernel(x, g_mix, w_in, b_f, g_q, w_q_up, g_kv, w_kv_up, w_o, g_mlp, w_ff1, w_ff2, g_final):
    b, seq, d = x.shape
    depth = w_in.shape[0]
    assert b == 1 and depth == 1, "single-sequence, single-layer problem"
    assert seq % PROJ_TM == 0 and seq % ATT_TQ == 0 and seq % OUT_TM == 0 and ATT_TQ % ATT_TK == 0
    out = _layer(x[0], g_mix[0], w_in[0], b_f[0], g_q[0], w_q_up[0], g_kv[0], w_kv_up[0], w_o[0],
                 g_mlp[0], w_ff1[0], w_ff2[0], g_final)
    return out[None]
```

```python
import math

import numpy as np
import jax
import jax.numpy as jnp
from jax import lax
from jax.experimental import pallas as pl
from jax.experimental.pallas import tpu as pltpu

F32 = jnp.float32
BF16 = jnp.bfloat16

EPS = 1e-6
CHUNK = 64
N_HEADS = 8
HEAD_DIM = 64
ROPE_DIM = 32
Q_LORA = 256
KV_LORA = 128
ROPE_THETA = 10000.0
LOG2E = math.log2(math.e)
FOX_SCALE = LOG2E / math.sqrt(HEAD_DIM)
MLA_SCALE = LOG2E / math.sqrt(HEAD_DIM + ROPE_DIM)

LANES = 128
BIAS_LANE = HEAD_DIM
MAX_JUMP = 60.0

PROJ_TM = 512
ATT_TQ = 512
ATT_TK = 512
OUT_TM = 512
FF_CHUNK = 1024

GROUP_W = N_HEADS * LANES

FOX_W = N_HEADS * HEAD_DIM
_C_FQ = 0
_C_FK = _C_FQ + FOX_W
_C_GATE = _C_FK + FOX_W
_C_KVLAT = _C_GATE + LANES
_C_QLAT = _C_KVLAT + KV_LORA
TERM_LANES = N_HEADS


def _rms(x, g):
    return x * lax.rsqrt(jnp.mean(x * x, axis=-1, keepdims=True) + EPS) * g


def _split3_packed(a, lane):
    hi = a.astype(BF16).astype(F32)
    r = a - hi
    mid = r.astype(BF16).astype(F32)
    lo = r - mid
    packed = jnp.where(lane < TERM_LANES, hi,
                       jnp.where(lane < 2 * TERM_LANES, pltpu.roll(mid, TERM_LANES, 1),
                                 jnp.where(lane < 3 * TERM_LANES, pltpu.roll(lo, 2 * TERM_LANES, 1), 0.0)))
    return packed.astype(BF16)


def _sum3_packed(a):
    return a + pltpu.roll(a, LANES - TERM_LANES, 1) + pltpu.roll(a, LANES - 2 * TERM_LANES, 1)


def _rope(x, cos, sin, first_half):
    rot = jnp.where(first_half, pltpu.roll(x, LANES - ROPE_DIM // 2, 1), pltpu.roll(x, ROPE_DIM // 2, 1))
    return x * cos + rot * sin


def _spread_pair(pair, low, fill):
    return jnp.where(low, pair, fill), jnp.where(low, pltpu.roll(pair, HEAD_DIM, 1), fill)


def _dot(a, b):
    return jnp.dot(a, b, preferred_element_type=F32)


def _dot_nt(a, b):
    return lax.dot_general(a, b, (((1,), (1,)), ((), ())), preferred_element_type=F32)


def _proj_kernel(x_ref, gmix_ref, win_ref, wvt_ref, bf_ref, gq_ref, wqup_ref, gkv_ref, wkup_ref,
                 wvupt_ref, cos_ref, sin_ref, tri_ref, place_ref, qbias_ref,
                 q_out, k_out, vt_out, carry_ref):
    tm = x_ref.shape[0]

    @pl.when(pl.program_id(0) == 0)
    def _():
        carry_ref[...] = jnp.zeros_like(carry_ref)

    hb = _rms(x_ref[...], gmix_ref[...]).astype(BF16)

    lane = lax.broadcasted_iota(jnp.int32, (tm, LANES), 1)
    low = lane < HEAD_DIM

    pq = _dot(hb, win_ref[:, _C_FQ:_C_FQ + FOX_W]) * FOX_SCALE
    for j in range(N_HEADS // 2):
        even, odd = _spread_pair(pq[:, j * LANES:(j + 1) * LANES], low, qbias_ref[...])
        q_out[:, 2 * j * LANES:(2 * j + 1) * LANES] = even.astype(BF16)
        q_out[:, (2 * j + 1) * LANES:(2 * j + 2) * LANES] = odd.astype(BF16)

    gk = _dot(hb, win_ref[:, _C_GATE:_C_GATE + LANES + KV_LORA])
    gate = gk[:, 0:LANES]
    z = gate + bf_ref[...]
    log_f = jnp.minimum(z, 0.0) - jnp.log1p(jnp.exp(-jnp.abs(z)))
    c = _sum3_packed(_dot(tri_ref[...], _split3_packed(log_f, lane))) + carry_ref[0:1, :]
    carry_ref[...] = jnp.broadcast_to(c[tm - 1:tm, :], carry_ref.shape)
    placed = _dot(_split3_packed(c * -LOG2E, lane), place_ref[...])
    pk = _dot(hb, win_ref[:, _C_FK:_C_FK + FOX_W])
    for j in range(N_HEADS // 2):
        even, odd = 2 * j * LANES, (2 * j + 1) * LANES
        pair = pk[:, j * LANES:(j + 1) * LANES]
        k_out[:, even:even + LANES] = jnp.where(low, pair, placed[:, even:even + LANES]).astype(BF16)
        k_out[:, odd:odd + LANES] = jnp.where(
            low, pltpu.roll(pair, HEAD_DIM, 1), placed[:, odd:odd + LANES]).astype(BF16)

    cos = cos_ref[...]
    sin = sin_ref[...]
    first_half = lane < HEAD_DIM + ROPE_DIM // 2

    q_lat = _dot(hb, win_ref[:, _C_QLAT:_C_QLAT + Q_LORA])
    nq = _rms(q_lat, gq_ref[...]).astype(BF16)
    cq = _dot(nq, wqup_ref[...])
    cos_q, sin_q = cos * MLA_SCALE, sin * MLA_SCALE
    for h in range(N_HEADS):
        xg = cq[:, h * LANES:(h + 1) * LANES]
        q_out[:, GROUP_W + h * LANES:GROUP_W + (h + 1) * LANES] = _rope(xg, cos_q, sin_q, first_half).astype(BF16)

    nkv = _rms(gk[:, LANES:LANES + KV_LORA], gkv_ref[...]).astype(BF16)
    kn = _dot(nkv, wkup_ref[...])
    kr = _rope(jnp.where(low, 0.0, gate), cos, sin, first_half)
    for h in range(N_HEADS):
        k_out[:, GROUP_W + h * LANES:GROUP_W + (h + 1) * LANES] = (kn[:, h * LANES:(h + 1) * LANES] + kr).astype(BF16)

    vf = _dot_nt(wvt_ref[...], hb).astype(BF16)
    vm = _dot_nt(wvupt_ref[...], nkv).astype(BF16)
    nv = vf.shape[0]
    for b in range(tm // ATT_TK):
        vt_out[b, 0:nv, :] = vf[:, b * ATT_TK:(b + 1) * ATT_TK]
        vt_out[b, nv:2 * nv, :] = vm[:, b * ATT_TK:(b + 1) * ATT_TK]


def _attn_kernel(q_ref, k_ref, vt_ref, dmask_ref, o_ref, p_ref, cs_ref, beta_ref, acc_ref, m_ref, l_ref, jump_ref,
                 flag_ref):
    tq = q_ref.shape[0]
    tk = vt_ref.shape[2]
    assert tq == tk
    i = pl.program_id(1)

    def qk(t, h):
        kh = k_ref[pl.ds(pl.multiple_of(t * tk, tk), tk), h * LANES:(h + 1) * LANES]
        return _dot_nt(kh, q_ref[:, h * LANES:(h + 1) * LANES])

    def diagonal_scores(h):
        return qk(i, h) + dmask_ref[0]

    l_ref[...] = jnp.zeros_like(l_ref)
    acc_ref[...] = jnp.zeros_like(acc_ref)

    def stage_a_diagonal():
        hk, hq = tk // 2, tq // 2
        assert hk % CHUNK == 0 and hk == hq
        for h in range(2):
            qh = q_ref[:, h * LANES:(h + 1) * LANES]
            k0 = pl.multiple_of(i * tk, tk)
            st_top = _dot_nt(k_ref[pl.ds(k0, hk), h * LANES:(h + 1) * LANES], qh) + dmask_ref[0, 0:hk, :]
            st_bot = (_dot_nt(k_ref[pl.ds(k0 + hk, hk), h * LANES:(h + 1) * LANES], qh[hq:, :])
                      + dmask_ref[0, hk:, hq:])
            m_early = jnp.max(st_top[:, :hq], axis=0, keepdims=True)
            m_late = jnp.maximum(jnp.max(st_top[:, hq:], axis=0, keepdims=True),
                                 jnp.max(st_bot, axis=0, keepdims=True))
            m0 = jnp.concatenate([m_early, m_late], axis=1)
            p_top = jnp.exp2(st_top - m0)
            p_bot = jnp.exp2(st_bot - m_late)
            p_ref[0, h, 0:hk, :] = p_top.astype(BF16)
            p_ref[0, h, hk:, 0:hq] = jnp.zeros((hk, hq), BF16)
            p_ref[0, h, hk:, hq:] = p_bot.astype(BF16)
            cs_ref[0, h] = jnp.concatenate(
                [jnp.sum(p_top[:, :hq], axis=0, keepdims=True),
                 jnp.sum(p_top[:, hq:], axis=0, keepdims=True) + jnp.sum(p_bot, axis=0, keepdims=True)], axis=1)
            beta_ref[0, h] = jnp.ones_like(m0)
            jump_ref[h] = jnp.zeros_like(m0)
            m_ref[h] = m0

    def stage_a(u, slot):
        for h in range(2):
            st = qk(i - u, h)
            m_old = m_ref[h]
            pt = jnp.exp2(st - m_old)
            p_ref[slot, h] = pt.astype(BF16)
            cs_ref[slot, h] = jnp.sum(pt, axis=0, keepdims=True)
            tile_max = jnp.max(st, axis=0, keepdims=True)
            m_new = jnp.maximum(m_old, tile_max)
            beta_ref[slot, h] = jnp.exp2(m_old - m_new)
            jump_ref[h] = jnp.maximum(jump_ref[h], tile_max - m_old)
            m_ref[h] = m_new

    def stage_u(u, slot):
        for h in range(2):
            beta = beta_ref[slot, h]
            acc_ref[h] = (acc_ref[h] + _dot(vt_ref[i - u], p_ref[slot, h])) * beta
            l_ref[h] = (l_ref[h] + cs_ref[slot, h]) * beta

    def full_tiles(u0, n):
        for k in range(n):
            stage_a(u0 + k + 1, (k + 1) % 2)
            stage_u(u0 + k, k % 2)

    stage_a_diagonal()

    def eight_tiles(jj, carry):
        full_tiles(8 * jj, 8)
        return carry

    lax.fori_loop(0, i // 8, eight_tiles, 0)

    @pl.when(i % 8 >= 4)
    def _():
        full_tiles(8 * (i // 8), 4)

    @pl.when(i % 4 >= 2)
    def _():
        full_tiles(4 * (i // 4), 2)

    def finish():
        o0 = acc_ref[0] * (1.0 / l_ref[0])
        o1 = acc_ref[1] * (1.0 / l_ref[1])
        row = lax.broadcasted_iota(jnp.int32, o0.shape, 0)
        ot = jnp.where(row < HEAD_DIM, o0, o1)
        o_ref[...] = ot.T.astype(BF16)

    @pl.when(i % 2 == 1)
    def _():
        full_tiles(i - 1, 1)
        flag_ref[0] = jnp.max(jump_ref[...])
        stage_u(i, 1)
        finish()

    @pl.when(i % 2 == 0)
    def _():
        flag_ref[0] = jnp.max(jump_ref[...])
        stage_u(i, 0)
        finish()

    @pl.when(flag_ref[0] > MAX_JUMP)
    def _():
        m_ref[...] = jnp.full_like(m_ref, -jnp.inf)
        l_ref[...] = jnp.zeros_like(l_ref)
        acc_ref[...] = jnp.zeros_like(acc_ref)

        def two_pass(t, st, h):
            m_prev = m_ref[h]
            m_new = jnp.maximum(m_prev, jnp.max(st, axis=0, keepdims=True))
            alpha = jnp.exp2(m_prev - m_new)
            pt = jnp.exp2(st - m_new)
            l_ref[h] = alpha * l_ref[h] + jnp.sum(pt, axis=0, keepdims=True)
            acc_ref[h] = alpha * acc_ref[h] + _dot(vt_ref[t], pt.astype(BF16))
            m_ref[h] = m_new

        def full_tile(t, carry):
            for h in range(2):
                two_pass(t, qk(t, h), h)
            return carry

        lax.fori_loop(0, i, full_tile, 0)
        for h in range(2):
            two_pass(i, diagonal_scores(h), h)
        finish()


def _out_kernel(o_ref, x_ref, wo_ref, gmlp_ref, w1_ref, w2_ref, gfin_ref, out_ref):
    x1 = x_ref[...] + _dot(o_ref[...], wo_ref[...])
    h2 = _rms(x1, gmlp_ref[...]).astype(BF16)
    y = x1
    for c in range(w1_ref.shape[1] // FF_CHUNK):
        u = _dot(h2, w1_ref[:, c * FF_CHUNK:(c + 1) * FF_CHUNK])
        a = jnp.square(jnp.maximum(u, 0.0)).astype(BF16)
        y = y + _dot(a, w2_ref[c * FF_CHUNK:(c + 1) * FF_CHUNK, :])
    out_ref[...] = _rms(y, gfin_ref[...])


def _pad_heads(w, head_w):
    k = w.shape[0]
    w3 = w.reshape(k, N_HEADS, head_w)
    return jnp.pad(w3, ((0, 0), (0, 0), (0, LANES - head_w))).reshape(k, GROUP_W)


def _rope_tables(seq):
    half = ROPE_DIM // 2
    inv = ROPE_THETA ** (-np.arange(half, dtype=np.float64) / half)
    ang = np.arange(seq, dtype=np.float64)[:, None] * inv[None, :]
    cos, sin = np.cos(ang), np.sin(ang)
    cos_t = np.ones((seq, LANES), np.float64)
    sin_t = np.zeros((seq, LANES), np.float64)
    cos_t[:, HEAD_DIM:HEAD_DIM + half] = cos
    cos_t[:, HEAD_DIM + half:HEAD_DIM + ROPE_DIM] = cos
    sin_t[:, HEAD_DIM:HEAD_DIM + half] = -sin
    sin_t[:, HEAD_DIM + half:HEAD_DIM + ROPE_DIM] = sin
    return jnp.asarray(cos_t, F32), jnp.asarray(sin_t, F32)


def _placement():
    e = np.zeros((LANES, GROUP_W), np.float32)
    for t in range(3):
        for h in range(N_HEADS):
            e[t * TERM_LANES + h, h * LANES + BIAS_LANE + t] = 1.0
    qb = np.zeros((1, LANES), np.float32)
    qb[0, BIAS_LANE:BIAS_LANE + 3] = 1.0
    return jnp.asarray(e, BF16), jnp.asarray(qb, F32)


def _diagonal_masks():
    kpos = np.arange(ATT_TK)[:, None]
    qpos = np.arange(ATT_TQ)[None, :]
    visible = np.stack([kpos <= qpos, kpos // CHUNK <= qpos // CHUNK])
    return jnp.asarray(np.where(visible, 0.0, -np.inf), F32)


def _resident(shape):
    nd = len(shape)
    return pl.BlockSpec(shape, lambda *_: (0,) * nd, pipeline_mode=pl.Buffered(1))


def _layer(x2, g_mix, w_in, b_f, g_q, w_q_up, g_kv, w_kv_up, w_o, g_mlp, w_ff1, w_ff2, g_out):
    seq, d = x2.shape
    fw = FOX_W
    splits = np.cumsum([fw, fw, fw, N_HEADS, Q_LORA, KV_LORA])
    w_fq, w_fk, w_fv, w_lg, w_ql, w_kvl, w_kr = jnp.split(w_in, splits, axis=1)
    w_kr_g = jnp.pad(w_kr, ((0, 0), (HEAD_DIM, LANES - HEAD_DIM - ROPE_DIM)))
    w_gate = w_kr_g.at[:, 0:N_HEADS].set(w_lg)
    win = jnp.concatenate([w_fq, w_fk, w_gate, w_kvl, w_ql], axis=1).astype(BF16)
    wvt = w_fv.T.astype(BF16)
    bf = jnp.pad(b_f, (0, LANES - N_HEADS)).reshape(1, LANES)
    wqup = _pad_heads(w_q_up, HEAD_DIM + ROPE_DIM).astype(BF16)
    wkv3 = w_kv_up.reshape(KV_LORA, N_HEADS, 2 * HEAD_DIM)
    wkup = _pad_heads(wkv3[:, :, :HEAD_DIM].reshape(KV_LORA, fw), HEAD_DIM).astype(BF16)
    wvupt = wkv3[:, :, HEAD_DIM:].reshape(KV_LORA, fw).T.astype(BF16)
    cos_t, sin_t = _rope_tables(seq)
    tri = jnp.asarray(np.tril(np.ones((PROJ_TM, PROJ_TM), np.float32)), BF16)
    place, qbias = _placement()

    n_kv = seq // ATT_TK
    row = lambda i: (i, 0)
    q_all, k_all, vt_all = pl.pallas_call(
        _proj_kernel,
        grid=(seq // PROJ_TM,),
        in_specs=[
            pl.BlockSpec((PROJ_TM, d), row),
            _resident((1, d)), _resident(win.shape), _resident(wvt.shape), _resident((1, LANES)),
            _resident((1, Q_LORA)), _resident(wqup.shape), _resident((1, KV_LORA)), _resident(wkup.shape),
            _resident(wvupt.shape),
            pl.BlockSpec((PROJ_TM, LANES), row), pl.BlockSpec((PROJ_TM, LANES), row),
            _resident(tri.shape), _resident(place.shape), _resident(qbias.shape),
        ],
        out_specs=[
            pl.BlockSpec((PROJ_TM, 2 * GROUP_W), row),
            pl.BlockSpec((PROJ_TM, 2 * GROUP_W), row),
            pl.BlockSpec((PROJ_TM // ATT_TK, 2 * fw, ATT_TK), lambda i: (i, 0, 0)),
        ],
        out_shape=[
            jax.ShapeDtypeStruct((seq, 2 * GROUP_W), BF16),
            jax.ShapeDtypeStruct((seq, 2 * GROUP_W), BF16),
            jax.ShapeDtypeStruct((n_kv, 2 * fw, ATT_TK), BF16),
        ],
        scratch_shapes=[pltpu.VMEM((8, LANES), F32)],
        compiler_params=pltpu.CompilerParams(
            dimension_semantics=("arbitrary",), vmem_limit_bytes=48 * 1024 * 1024),
        name="proj",
    )(x2, g_mix.reshape(1, d), win, wvt, bf, g_q.reshape(1, Q_LORA), wqup, g_kv.reshape(1, KV_LORA),
      wkup, wvupt, cos_t, sin_t, tri, place, qbias)

    n_pairs = N_HEADS
    o_all = pl.pallas_call(
        _attn_kernel,
        grid=(n_pairs, seq // ATT_TQ),
        in_specs=[
            pl.BlockSpec((ATT_TQ, 2 * LANES), lambda p, i: (i, p)),
            pl.BlockSpec((seq, 2 * LANES), lambda p, i: (0, p)),
            pl.BlockSpec((n_kv, LANES, ATT_TK), lambda p, i: (0, p, 0)),
            pl.BlockSpec((1, ATT_TK, ATT_TQ), lambda p, i: (p // (N_HEADS // 2), 0, 0)),
        ],
        out_specs=pl.BlockSpec((ATT_TQ, LANES), lambda p, i: (i, p)),
        out_shape=jax.ShapeDtypeStruct((seq, 2 * fw), BF16),
        scratch_shapes=[
            pltpu.VMEM((2, 2, ATT_TK, ATT_TQ), BF16),
            pltpu.VMEM((2, 2, 1, ATT_TQ), F32),
            pltpu.VMEM((2, 2, 1, ATT_TQ), F32),
            pltpu.VMEM((2, LANES, ATT_TQ), F32),
            pltpu.VMEM((2, 1, ATT_TQ), F32),
            pltpu.VMEM((2, 1, ATT_TQ), F32),
            pltpu.VMEM((2, 1, ATT_TQ), F32),
            pltpu.SMEM((1,), F32),
        ],
        compiler_params=pltpu.CompilerParams(
            dimension_semantics=("arbitrary", "arbitrary"), vmem_limit_bytes=48 * 1024 * 1024),
        name="attn",
    )(q_all, k_all, vt_all, _diagonal_masks())

    d_ff = w_ff1.shape[1]
    return pl.pallas_call(
        _out_kernel,
        grid=(seq // OUT_TM,),
        in_specs=[
            pl.BlockSpec((OUT_TM, 2 * fw), row),
            pl.BlockSpec((OUT_TM, d), row),
            _resident((2 * fw, d)), _resident((1, d)), _resident((d, d_ff)), _resident((d_ff, d)),
            _resident((1, d)),
        ],
        out_specs=pl.BlockSpec((OUT_TM, d), row),
        out_shape=jax.ShapeDtypeStruct((seq, d), F32),
        compiler_params=pltpu.CompilerParams(
            dimension_semantics=("arbitrary",), vmem_limit_bytes=56 * 1024 * 1024),
        name="out_mlp",
    )(o_all, x2, w_o.astype(BF16), g_mlp.reshape(1, d), w_ff1.astype(BF16), w_ff2.astype(BF16),
      g_out.reshape(1, d))


def kernel(x, g_mix, w_in, b_f, g_q, w_q_up, g_kv, w_kv_up, w_o, g_mlp, w_ff1, w_ff2, g_final):
    b, seq, d = x.shape
    depth = w_in.shape[0]
    assert b == 1 and depth == 1, "single-sequence, single-layer problem"
    assert seq % PROJ_TM == 0 and seq % ATT_TQ == 0 and seq % OUT_TM == 0 and ATT_TQ % ATT_TK == 0
    out = _layer(x[0], g_mix[0], w_in[0], b_f[0], g_q[0], w_q_up[0], g_kv[0], w_kv_up[0], w_o[0],
                 g_mlp[0], w_ff1[0], w_ff2[0], g_final)
    return out[None]
```

```python
import math

import numpy as np
import jax
import jax.numpy as jnp
from jax import lax
from jax.experimental import pallas as pl
from jax.experimental.pallas import tpu as pltpu

F32 = jnp.float32
BF16 = jnp.bfloat16

EPS = 1e-6
CHUNK = 64
N_HEADS = 8
HEAD_DIM = 64
ROPE_DIM = 32
Q_LORA = 256
KV_LORA = 128
ROPE_THETA = 10000.0
LOG2E = math.log2(math.e)
FOX_SCALE = LOG2E / math.sqrt(HEAD_DIM)
MLA_SCALE = LOG2E / math.sqrt(HEAD_DIM + ROPE_DIM)

LANES = 128
BIAS_LANE = HEAD_DIM
MAX_JUMP = 60.0

PROJ_TM = 512
ATT_TQ = 512
ATT_TK = 512
ATT_SUB = 4
OUT_TM = 512
FF_CHUNK = 1024

GROUP_W = N_HEADS * LANES

FOX_W = N_HEADS * HEAD_DIM
_C_FQ = 0
_C_FK = _C_FQ + FOX_W
_C_GATE = _C_FK + FOX_W
_C_KVLAT = _C_GATE + LANES
_C_QLAT = _C_KVLAT + KV_LORA
TERM_LANES = N_HEADS


def _rms(x, g):
    return x * lax.rsqrt(jnp.mean(x * x, axis=-1, keepdims=True) + EPS) * g


def _split3_packed(a, lane):
    hi = a.astype(BF16).astype(F32)
    r = a - hi
    mid = r.astype(BF16).astype(F32)
    lo = r - mid
    packed = jnp.where(lane < TERM_LANES, hi,
                       jnp.where(lane < 2 * TERM_LANES, pltpu.roll(mid, TERM_LANES, 1),
                                 jnp.where(lane < 3 * TERM_LANES, pltpu.roll(lo, 2 * TERM_LANES, 1), 0.0)))
    return packed.astype(BF16)


def _sum3_packed(a):
    return a + pltpu.roll(a, LANES - TERM_LANES, 1) + pltpu.roll(a, LANES - 2 * TERM_LANES, 1)


def _rope(x, cos, sin, first_half):
    rot = jnp.where(first_half, pltpu.roll(x, LANES - ROPE_DIM // 2, 1), pltpu.roll(x, ROPE_DIM // 2, 1))
    return x * cos + rot * sin


def _spread_pair(pair, low, fill):
    return jnp.where(low, pair, fill), jnp.where(low, pltpu.roll(pair, HEAD_DIM, 1), fill)


def _dot(a, b):
    return jnp.dot(a, b, preferred_element_type=F32)


def _dot_nt(a, b):
    return lax.dot_general(a, b, (((1,), (1,)), ((), ())), preferred_element_type=F32)


def _proj_kernel(x_ref, gmix_ref, win_ref, wvt_ref, bf_ref, gq_ref, wqup_ref, gkv_ref, wkup_ref,
                 wvupt_ref, cos_ref, sin_ref, tri_ref, place_ref, qbias_ref,
                 q_out, k_out, vt_out, carry_ref):
    tm = x_ref.shape[0]

    @pl.when(pl.program_id(0) == 0)
    def _():
        carry_ref[...] = jnp.zeros_like(carry_ref)

    hb = _rms(x_ref[...], gmix_ref[...]).astype(BF16)

    lane = lax.broadcasted_iota(jnp.int32, (tm, LANES), 1)
    low = lane < HEAD_DIM

    pq = _dot(hb, win_ref[:, _C_FQ:_C_FQ + FOX_W]) * FOX_SCALE
    for j in range(N_HEADS // 2):
        even, odd = _spread_pair(pq[:, j * LANES:(j + 1) * LANES], low, qbias_ref[...])
        q_out[:, 2 * j * LANES:(2 * j + 1) * LANES] = even.astype(BF16)
        q_out[:, (2 * j + 1) * LANES:(2 * j + 2) * LANES] = odd.astype(BF16)

    gk = _dot(hb, win_ref[:, _C_GATE:_C_GATE + LANES + KV_LORA])
    gate = gk[:, 0:LANES]
    z = gate + bf_ref[...]
    log_f = jnp.minimum(z, 0.0) - jnp.log1p(jnp.exp(-jnp.abs(z)))
    c = _sum3_packed(_dot(tri_ref[...], _split3_packed(log_f, lane))) + carry_ref[0:1, :]
    carry_ref[...] = jnp.broadcast_to(c[tm - 1:tm, :], carry_ref.shape)
    placed = _dot(_split3_packed(c * -LOG2E, lane), place_ref[...])
    pk = _dot(hb, win_ref[:, _C_FK:_C_FK + FOX_W])
    for j in range(N_HEADS // 2):
        even, odd = 2 * j * LANES, (2 * j + 1) * LANES
        pair = pk[:, j * LANES:(j + 1) * LANES]
        k_out[:, even:even + LANES] = jnp.where(low, pair, placed[:, even:even + LANES]).astype(BF16)
        k_out[:, odd:odd + LANES] = jnp.where(
            low, pltpu.roll(pair, HEAD_DIM, 1), placed[:, odd:odd + LANES]).astype(BF16)

    cos = cos_ref[...]
    sin = sin_ref[...]
    first_half = lane < HEAD_DIM + ROPE_DIM // 2

    q_lat = _dot(hb, win_ref[:, _C_QLAT:_C_QLAT + Q_LORA])
    nq = _rms(q_lat, gq_ref[...]).astype(BF16)
    cq = _dot(nq, wqup_ref[...])
    cos_q, sin_q = cos * MLA_SCALE, sin * MLA_SCALE
    for h in range(N_HEADS):
        xg = cq[:, h * LANES:(h + 1) * LANES]
        q_out[:, GROUP_W + h * LANES:GROUP_W + (h + 1) * LANES] = _rope(xg, cos_q, sin_q, first_half).astype(BF16)

    nkv = _rms(gk[:, LANES:LANES + KV_LORA], gkv_ref[...]).astype(BF16)
    kn = _dot(nkv, wkup_ref[...])
    kr = _rope(jnp.where(low, 0.0, gate), cos, sin, first_half)
    for h in range(N_HEADS):
        k_out[:, GROUP_W + h * LANES:GROUP_W + (h + 1) * LANES] = (kn[:, h * LANES:(h + 1) * LANES] + kr).astype(BF16)

    vf = _dot_nt(wvt_ref[...], hb).astype(BF16)
    vm = _dot_nt(wvupt_ref[...], nkv).astype(BF16)
    nv = vf.shape[0]
    for b in range(tm // ATT_TK):
        vt_out[b, 0:nv, :] = vf[:, b * ATT_TK:(b + 1) * ATT_TK]
        vt_out[b, nv:2 * nv, :] = vm[:, b * ATT_TK:(b + 1) * ATT_TK]


def _attn_kernel(q_ref, k_ref, vt_ref, dmask_ref, o_ref, p_ref, cs_ref, beta_ref, acc_ref, m_ref, l_ref, jump_ref,
                 flag_ref):
    tk = vt_ref.shape[2]
    tq = tk
    n_sub = q_ref.shape[0] // tq

    def query_tile(sub, carry):
        i = pl.program_id(1) * n_sub + sub
        rows = pl.ds(pl.multiple_of(sub * tq, tq), tq)

        def qk(t, h):
            kh = k_ref[pl.ds(pl.multiple_of(t * tk, tk), tk), h * LANES:(h + 1) * LANES]
            return _dot_nt(kh, q_ref[rows, h * LANES:(h + 1) * LANES])

        def diagonal_scores(h):
            return qk(i, h) + dmask_ref[0]

        l_ref[...] = jnp.zeros_like(l_ref)
        acc_ref[...] = jnp.zeros_like(acc_ref)

        def stage_a_diagonal():
            hk, hq = tk // 2, tq // 2
            assert hk % CHUNK == 0 and hk == hq
            for h in range(2):
                qh = q_ref[rows, h * LANES:(h + 1) * LANES]
                k0 = pl.multiple_of(i * tk, tk)
                st_top = _dot_nt(k_ref[pl.ds(k0, hk), h * LANES:(h + 1) * LANES], qh) + dmask_ref[0, 0:hk, :]
                st_bot = (_dot_nt(k_ref[pl.ds(k0 + hk, hk), h * LANES:(h + 1) * LANES], qh[hq:, :])
                          + dmask_ref[0, hk:, hq:])
                m_early = jnp.max(st_top[:, :hq], axis=0, keepdims=True)
                m_late = jnp.maximum(jnp.max(st_top[:, hq:], axis=0, keepdims=True),
                                     jnp.max(st_bot, axis=0, keepdims=True))
                m0 = jnp.concatenate([m_early, m_late], axis=1)
                p_top = jnp.exp2(st_top - m0)
                p_bot = jnp.exp2(st_bot - m_late)
                p_ref[0, h, 0:hk, :] = p_top.astype(BF16)
                p_ref[0, h, hk:, 0:hq] = jnp.zeros((hk, hq), BF16)
                p_ref[0, h, hk:, hq:] = p_bot.astype(BF16)
                cs_ref[0, h] = jnp.concatenate(
                    [jnp.sum(p_top[:, :hq], axis=0, keepdims=True),
                     jnp.sum(p_top[:, hq:], axis=0, keepdims=True) + jnp.sum(p_bot, axis=0, keepdims=True)], axis=1)
                beta_ref[0, h] = jnp.ones_like(m0)
                jump_ref[h] = jnp.zeros_like(m0)
                m_ref[h] = m0

        def stage_a(u, slot):
            for h in range(2):
                st = qk(i - u, h)
                m_old = m_ref[h]
                pt = jnp.exp2(st - m_old)
                p_ref[slot, h] = pt.astype(BF16)
                cs_ref[slot, h] = jnp.sum(pt, axis=0, keepdims=True)
                tile_max = jnp.max(st, axis=0, keepdims=True)
                m_new = jnp.maximum(m_old, tile_max)
                beta_ref[slot, h] = jnp.exp2(m_old - m_new)
                jump_ref[h] = jnp.maximum(jump_ref[h], tile_max - m_old)
                m_ref[h] = m_new

        def stage_u(u, slot):
            for h in range(2):
                beta = beta_ref[slot, h]
                acc_ref[h] = (acc_ref[h] + _dot(vt_ref[i - u], p_ref[slot, h])) * beta
                l_ref[h] = (l_ref[h] + cs_ref[slot, h]) * beta

        def full_tiles(u0, n):
            for k in range(n):
                stage_a(u0 + k + 1, (k + 1) % 2)
                stage_u(u0 + k, k % 2)

        stage_a_diagonal()

        def eight_tiles(jj, inner):
            full_tiles(8 * jj, 8)
            return inner

        lax.fori_loop(0, i // 8, eight_tiles, 0)

        @pl.when(i % 8 >= 4)
        def _():
            full_tiles(8 * (i // 8), 4)

        @pl.when(i % 4 >= 2)
        def _():
            full_tiles(4 * (i // 4), 2)

        def finish():
            o0 = acc_ref[0] * (1.0 / l_ref[0])
            o1 = acc_ref[1] * (1.0 / l_ref[1])
            row = lax.broadcasted_iota(jnp.int32, o0.shape, 0)
            ot = jnp.where(row < HEAD_DIM, o0, o1)
            o_ref[rows, :] = ot.T.astype(BF16)

        @pl.when(i % 2 == 1)
        def _():
            full_tiles(i - 1, 1)
            flag_ref[0] = jnp.max(jump_ref[...])
            stage_u(i, 1)
            finish()

        @pl.when(i % 2 == 0)
        def _():
            flag_ref[0] = jnp.max(jump_ref[...])
            stage_u(i, 0)
            finish()

        @pl.when(flag_ref[0] > MAX_JUMP)
        def _():
            m_ref[...] = jnp.full_like(m_ref, -jnp.inf)
            l_ref[...] = jnp.zeros_like(l_ref)
            acc_ref[...] = jnp.zeros_like(acc_ref)

            def two_pass(t, st, h):
                m_prev = m_ref[h]
                m_new = jnp.maximum(m_prev, jnp.max(st, axis=0, keepdims=True))
                alpha = jnp.exp2(m_prev - m_new)
                pt = jnp.exp2(st - m_new)
                l_ref[h] = alpha * l_ref[h] + jnp.sum(pt, axis=0, keepdims=True)
                acc_ref[h] = alpha * acc_ref[h] + _dot(vt_ref[t], pt.astype(BF16))
                m_ref[h] = m_new

            def full_tile(t, inner):
                for h in range(2):
                    two_pass(t, qk(t, h), h)
                return inner

            lax.fori_loop(0, i, full_tile, 0)
            for h in range(2):
                two_pass(i, diagonal_scores(h), h)
            finish()

        return carry

    lax.fori_loop(0, n_sub, query_tile, 0)


def _out_kernel(o_ref, x_ref, wo_ref, gmlp_ref, w1_ref, w2_ref, gfin_ref, out_ref):
    x1 = x_ref[...] + _dot(o_ref[...], wo_ref[...])
    h2 = _rms(x1, gmlp_ref[...]).astype(BF16)
    y = x1
    for c in range(w1_ref.shape[1] // FF_CHUNK):
        u = _dot(h2, w1_ref[:, c * FF_CHUNK:(c + 1) * FF_CHUNK])
        a = jnp.square(jnp.maximum(u, 0.0)).astype(BF16)
        y = y + _dot(a, w2_ref[c * FF_CHUNK:(c + 1) * FF_CHUNK, :])
    out_ref[...] = _rms(y, gfin_ref[...])


def _pad_heads(w, head_w):
    k = w.shape[0]
    w3 = w.reshape(k, N_HEADS, head_w)
    return jnp.pad(w3, ((0, 0), (0, 0), (0, LANES - head_w))).reshape(k, GROUP_W)


def _rope_tables(seq):
    half = ROPE_DIM // 2
    inv = ROPE_THETA ** (-np.arange(half, dtype=np.float64) / half)
    ang = np.arange(seq, dtype=np.float64)[:, None] * inv[None, :]
    cos, sin = np.cos(ang), np.sin(ang)
    cos_t = np.ones((seq, LANES), np.float64)
    sin_t = np.zeros((seq, LANES), np.float64)
    cos_t[:, HEAD_DIM:HEAD_DIM + half] = cos
    cos_t[:, HEAD_DIM + half:HEAD_DIM + ROPE_DIM] = cos
    sin_t[:, HEAD_DIM:HEAD_DIM + half] = -sin
    sin_t[:, HEAD_DIM + half:HEAD_DIM + ROPE_DIM] = sin
    return jnp.asarray(cos_t, F32), jnp.asarray(sin_t, F32)


def _placement():
    e = np.zeros((LANES, GROUP_W), np.float32)
    for t in range(3):
        for h in range(N_HEADS):
            e[t * TERM_LANES + h, h * LANES + BIAS_LANE + t] = 1.0
    qb = np.zeros((1, LANES), np.float32)
    qb[0, BIAS_LANE:BIAS_LANE + 3] = 1.0
    return jnp.asarray(e, BF16), jnp.asarray(qb, F32)


def _diagonal_masks():
    kpos = np.arange(ATT_TK)[:, None]
    qpos = np.arange(ATT_TQ)[None, :]
    visible = np.stack([kpos <= qpos, kpos // CHUNK <= qpos // CHUNK])
    return jnp.asarray(np.where(visible, 0.0, -np.inf), F32)


def _resident(shape):
    nd = len(shape)
    return pl.BlockSpec(shape, lambda *_: (0,) * nd, pipeline_mode=pl.Buffered(1))


def _layer(x2, g_mix, w_in, b_f, g_q, w_q_up, g_kv, w_kv_up, w_o, g_mlp, w_ff1, w_ff2, g_out):
    seq, d = x2.shape
    fw = FOX_W
    splits = np.cumsum([fw, fw, fw, N_HEADS, Q_LORA, KV_LORA])
    w_fq, w_fk, w_fv, w_lg, w_ql, w_kvl, w_kr = jnp.split(w_in, splits, axis=1)
    w_kr_g = jnp.pad(w_kr, ((0, 0), (HEAD_DIM, LANES - HEAD_DIM - ROPE_DIM)))
    w_gate = w_kr_g.at[:, 0:N_HEADS].set(w_lg)
    win = jnp.concatenate([w_fq, w_fk, w_gate, w_kvl, w_ql], axis=1).astype(BF16)
    wvt = w_fv.T.astype(BF16)
    bf = jnp.pad(b_f, (0, LANES - N_HEADS)).reshape(1, LANES)
    wqup = _pad_heads(w_q_up, HEAD_DIM + ROPE_DIM).astype(BF16)
    wkv3 = w_kv_up.reshape(KV_LORA, N_HEADS, 2 * HEAD_DIM)
    wkup = _pad_heads(wkv3[:, :, :HEAD_DIM].reshape(KV_LORA, fw), HEAD_DIM).astype(BF16)
    wvupt = wkv3[:, :, HEAD_DIM:].reshape(KV_LORA, fw).T.astype(BF16)
    cos_t, sin_t = _rope_tables(seq)
    tri = jnp.asarray(np.tril(np.ones((PROJ_TM, PROJ_TM), np.float32)), BF16)
    place, qbias = _placement()

    n_kv = seq // ATT_TK
    row = lambda i: (i, 0)
    q_all, k_all, vt_all = pl.pallas_call(
        _proj_kernel,
        grid=(seq // PROJ_TM,),
        in_specs=[
            pl.BlockSpec((PROJ_TM, d), row),
            _resident((1, d)), _resident(win.shape), _resident(wvt.shape), _resident((1, LANES)),
            _resident((1, Q_LORA)), _resident(wqup.shape), _resident((1, KV_LORA)), _resident(wkup.shape),
            _resident(wvupt.shape),
            pl.BlockSpec((PROJ_TM, LANES), row), pl.BlockSpec((PROJ_TM, LANES), row),
            _resident(tri.shape), _resident(place.shape), _resident(qbias.shape),
        ],
        out_specs=[
            pl.BlockSpec((PROJ_TM, 2 * GROUP_W), row),
            pl.BlockSpec((PROJ_TM, 2 * GROUP_W), row),
            pl.BlockSpec((PROJ_TM // ATT_TK, 2 * fw, ATT_TK), lambda i: (i, 0, 0)),
        ],
        out_shape=[
            jax.ShapeDtypeStruct((seq, 2 * GROUP_W), BF16),
            jax.ShapeDtypeStruct((seq, 2 * GROUP_W), BF16),
            jax.ShapeDtypeStruct((n_kv, 2 * fw, ATT_TK), BF16),
        ],
        scratch_shapes=[pltpu.VMEM((8, LANES), F32)],
        compiler_params=pltpu.CompilerParams(
            dimension_semantics=("arbitrary",), vmem_limit_bytes=48 * 1024 * 1024),
        name="proj",
    )(x2, g_mix.reshape(1, d), win, wvt, bf, g_q.reshape(1, Q_LORA), wqup, g_kv.reshape(1, KV_LORA),
      wkup, wvupt, cos_t, sin_t, tri, place, qbias)

    n_pairs = N_HEADS
    q_rows = ATT_TQ * ATT_SUB
    o_all = pl.pallas_call(
        _attn_kernel,
        grid=(n_pairs, seq // q_rows),
        in_specs=[
            pl.BlockSpec((q_rows, 2 * LANES), lambda p, i: (i, p)),
            pl.BlockSpec((seq, 2 * LANES), lambda p, i: (0, p)),
            pl.BlockSpec((n_kv, LANES, ATT_TK), lambda p, i: (0, p, 0)),
            pl.BlockSpec((1, ATT_TK, ATT_TQ), lambda p, i: (p // (N_HEADS // 2), 0, 0)),
        ],
        out_specs=pl.BlockSpec((q_rows, LANES), lambda p, i: (i, p)),
        out_shape=jax.ShapeDtypeStruct((seq, 2 * fw), BF16),
        scratch_shapes=[
            pltpu.VMEM((2, 2, ATT_TK, ATT_TQ), BF16),
            pltpu.VMEM((2, 2, 1, ATT_TQ), F32),
            pltpu.VMEM((2, 2, 1, ATT_TQ), F32),
            pltpu.VMEM((2, LANES, ATT_TQ), F32),
            pltpu.VMEM((2, 1, ATT_TQ), F32),
            pltpu.VMEM((2, 1, ATT_TQ), F32),
            pltpu.VMEM((2, 1, ATT_TQ), F32),
            pltpu.SMEM((1,), F32),
        ],
        compiler_params=pltpu.CompilerParams(
            dimension_semantics=("arbitrary", "arbitrary"), vmem_limit_bytes=48 * 1024 * 1024),
        name="attn",
    )(q_all, k_all, vt_all, _diagonal_masks())

    d_ff = w_ff1.shape[1]
    return pl.pallas_call(
        _out_kernel,
        grid=(seq // OUT_TM,),
        in_specs=[
            pl.BlockSpec((OUT_TM, 2 * fw), row),
            pl.BlockSpec((OUT_TM, d), row),
            _resident((2 * fw, d)), _resident((1, d)), _resident((d, d_ff)), _resident((d_ff, d)),
            _resident((1, d)),
        ],
        out_specs=pl.BlockSpec((OUT_TM, d), row),
        out_shape=jax.ShapeDtypeStruct((seq, d), F32),
        compiler_params=pltpu.CompilerParams(
            dimension_semantics=("arbitrary",), vmem_limit_bytes=56 * 1024 * 1024),
        name="out_mlp",
    )(o_all, x2, w_o.astype(BF16), g_mlp.reshape(1, d), w_ff1.astype(BF16), w_ff2.astype(BF16),
      g_out.reshape(1, d))


def kernel(x, g_mix, w_in, b_f, g_q, w_q_up, g_kv, w_kv_up, w_o, g_mlp, w_ff1, w_ff2, g_final):
    b, seq, d = x.shape
    depth = w_in.shape[0]
    assert b == 1 and depth == 1, "single-sequence, single-layer problem"
    assert seq % PROJ_TM == 0 and seq % (ATT_TQ * ATT_SUB) == 0 and seq % OUT_TM == 0 and ATT_TQ == ATT_TK
    out = _layer(x[0], g_mix[0], w_in[0], b_f[0], g_q[0], w_q_up[0], g_kv[0], w_kv_up[0], w_o[0],
                 g_mlp[0], w_ff1[0], w_ff2[0], g_final)
    return out[None]
```

```python
import math

import numpy as np
import jax
import jax.numpy as jnp
from jax import lax
from jax.experimental import pallas as pl
from jax.experimental.pallas import tpu as pltpu

F32 = jnp.float32
BF16 = jnp.bfloat16

EPS = 1e-6
CHUNK = 64
N_HEADS = 8
HEAD_DIM = 64
ROPE_DIM = 32
Q_LORA = 256
KV_LORA = 128
ROPE_THETA = 10000.0
LOG2E = math.log2(math.e)
FOX_SCALE = LOG2E / math.sqrt(HEAD_DIM)
MLA_SCALE = LOG2E / math.sqrt(HEAD_DIM + ROPE_DIM)

LANES = 128
BIAS_LANE = HEAD_DIM
MAX_JUMP = 60.0

PROJ_TM = 512
ATT_TQ = 512
ATT_TK = 512
OUT_TM = 512
FF_CHUNK = 1024

GROUP_W = N_HEADS * LANES

FOX_W = N_HEADS * HEAD_DIM
_C_FQ = 0
_C_FK = _C_FQ + FOX_W
_C_GATE = _C_FK + FOX_W
_C_KVLAT = _C_GATE + LANES
_C_QLAT = _C_KVLAT + KV_LORA
TERM_LANES = N_HEADS


def _rms(x, g):
    return x * lax.rsqrt(jnp.mean(x * x, axis=-1, keepdims=True) + EPS) * g


def _split3_packed(a, lane):
    hi = a.astype(BF16).astype(F32)
    r = a - hi
    mid = r.astype(BF16).astype(F32)
    lo = r - mid
    packed = jnp.where(lane < TERM_LANES, hi,
                       jnp.where(lane < 2 * TERM_LANES, pltpu.roll(mid, TERM_LANES, 1),
                                 jnp.where(lane < 3 * TERM_LANES, pltpu.roll(lo, 2 * TERM_LANES, 1), 0.0)))
    return packed.astype(BF16)


def _sum3_packed(a):
    return a + pltpu.roll(a, LANES - TERM_LANES, 1) + pltpu.roll(a, LANES - 2 * TERM_LANES, 1)


def _rope(x, cos, sin, first_half):
    rot = jnp.where(first_half, pltpu.roll(x, LANES - ROPE_DIM // 2, 1), pltpu.roll(x, ROPE_DIM // 2, 1))
    return x * cos + rot * sin


def _spread_pair(pair, low, fill):
    return jnp.where(low, pair, fill), jnp.where(low, pltpu.roll(pair, HEAD_DIM, 1), fill)


def _dot(a, b):
    return jnp.dot(a, b, preferred_element_type=F32)


def _dot_nt(a, b):
    return lax.dot_general(a, b, (((1,), (1,)), ((), ())), preferred_element_type=F32)


def _proj_kernel(x_ref, gmix_ref, win_ref, wvt_ref, bf_ref, gq_ref, wqup_ref, gkv_ref, wkup_ref,
                 wvupt_ref, cos_ref, sin_ref, tri_ref, place_ref, qbias_ref,
                 q_out, k_out, vt_out, carry_ref):
    tm = x_ref.shape[0]

    @pl.when(pl.program_id(0) == 0)
    def _():
        carry_ref[...] = jnp.zeros_like(carry_ref)

    hb = _rms(x_ref[...], gmix_ref[...]).astype(BF16)

    lane = lax.broadcasted_iota(jnp.int32, (tm, LANES), 1)
    low = lane < HEAD_DIM

    pq = _dot(hb, win_ref[:, _C_FQ:_C_FQ + FOX_W]) * FOX_SCALE
    for j in range(N_HEADS // 2):
        even, odd = _spread_pair(pq[:, j * LANES:(j + 1) * LANES], low, qbias_ref[...])
        q_out[:, 2 * j * LANES:(2 * j + 1) * LANES] = even.astype(BF16)
        q_out[:, (2 * j + 1) * LANES:(2 * j + 2) * LANES] = odd.astype(BF16)

    gk = _dot(hb, win_ref[:, _C_GATE:_C_GATE + LANES + KV_LORA])
    gate = gk[:, 0:LANES]
    z = gate + bf_ref[...]
    log_f = jnp.minimum(z, 0.0) - jnp.log1p(jnp.exp(-jnp.abs(z)))
    c = _sum3_packed(_dot(tri_ref[...], _split3_packed(log_f, lane))) + carry_ref[0:1, :]
    carry_ref[...] = jnp.broadcast_to(c[tm - 1:tm, :], carry_ref.shape)
    placed = _dot(_split3_packed(c * -LOG2E, lane), place_ref[...])
    pk = _dot(hb, win_ref[:, _C_FK:_C_FK + FOX_W])
    for j in range(N_HEADS // 2):
        even, odd = 2 * j * LANES, (2 * j + 1) * LANES
        pair = pk[:, j * LANES:(j + 1) * LANES]
        k_out[:, even:even + LANES] = jnp.where(low, pair, placed[:, even:even + LANES]).astype(BF16)
        k_out[:, odd:odd + LANES] = jnp.where(
            low, pltpu.roll(pair, HEAD_DIM, 1), placed[:, odd:odd + LANES]).astype(BF16)

    cos = cos_ref[...]
    sin = sin_ref[...]
    first_half = lane < HEAD_DIM + ROPE_DIM // 2

    q_lat = _dot(hb, win_ref[:, _C_QLAT:_C_QLAT + Q_LORA])
    nq = _rms(q_lat, gq_ref[...]).astype(BF16)
    cq = _dot(nq, wqup_ref[...])
    cos_q, sin_q = cos * MLA_SCALE, sin * MLA_SCALE
    for h in range(N_HEADS):
        xg = cq[:, h * LANES:(h + 1) * LANES]
        q_out[:, GROUP_W + h * LANES:GROUP_W + (h + 1) * LANES] = _rope(xg, cos_q, sin_q, first_half).astype(BF16)

    nkv = _rms(gk[:, LANES:LANES + KV_LORA], gkv_ref[...]).astype(BF16)
    kn = _dot(nkv, wkup_ref[...])
    kr = _rope(jnp.where(low, 0.0, gate), cos, sin, first_half)
    for h in range(N_HEADS):
        k_out[:, GROUP_W + h * LANES:GROUP_W + (h + 1) * LANES] = (kn[:, h * LANES:(h + 1) * LANES] + kr).astype(BF16)

    vf = _dot_nt(wvt_ref[...], hb).astype(BF16)
    vm = _dot_nt(wvupt_ref[...], nkv).astype(BF16)
    nv = vf.shape[0]
    for b in range(tm // ATT_TK):
        vt_out[b, 0:nv, :] = vf[:, b * ATT_TK:(b + 1) * ATT_TK]
        vt_out[b, nv:2 * nv, :] = vm[:, b * ATT_TK:(b + 1) * ATT_TK]


def _attn_kernel(q_ref, k_ref, vt_ref, dmask_ref, o_ref, p_ref, cs_ref, beta_ref, acc_ref, m_ref, l_ref, jump_ref,
                 flag_ref):
    tq = q_ref.shape[0]
    tk = vt_ref.shape[2]
    assert tq == tk
    i = pl.program_id(1)

    def qk(t, h):
        kh = k_ref[pl.ds(pl.multiple_of(t * tk, tk), tk), h * LANES:(h + 1) * LANES]
        return _dot_nt(kh, q_ref[:, h * LANES:(h + 1) * LANES])

    def diagonal_scores(h):
        return qk(i, h) + dmask_ref[0]

    l_ref[...] = jnp.zeros_like(l_ref)
    acc_ref[...] = jnp.zeros_like(acc_ref)

    def stage_a_diagonal():
        hk, hq = tk // 2, tq // 2
        assert hk % CHUNK == 0 and hk == hq
        for h in range(2):
            qh = q_ref[:, h * LANES:(h + 1) * LANES]
            k0 = pl.multiple_of(i * tk, tk)
            st_top = _dot_nt(k_ref[pl.ds(k0, hk), h * LANES:(h + 1) * LANES], qh) + dmask_ref[0, 0:hk, :]
            st_bot = (_dot_nt(k_ref[pl.ds(k0 + hk, hk), h * LANES:(h + 1) * LANES], qh[hq:, :])
                      + dmask_ref[0, hk:, hq:])
            m_early = jnp.max(st_top[:, :hq], axis=0, keepdims=True)
            m_late = jnp.maximum(jnp.max(st_top[:, hq:], axis=0, keepdims=True),
                                 jnp.max(st_bot, axis=0, keepdims=True))
            m0 = jnp.concatenate([m_early, m_late], axis=1)
            p_top = jnp.exp2(st_top - m0)
            p_bot = jnp.exp2(st_bot - m_late)
            p_ref[0, h, 0:hk, :] = p_top.astype(BF16)
            p_ref[0, h, hk:, 0:hq] = jnp.zeros((hk, hq), BF16)
            p_ref[0, h, hk:, hq:] = p_bot.astype(BF16)
            cs_ref[0, h] = jnp.concatenate(
                [jnp.sum(p_top[:, :hq], axis=0, keepdims=True),
                 jnp.sum(p_top[:, hq:], axis=0, keepdims=True) + jnp.sum(p_bot, axis=0, keepdims=True)], axis=1)
            beta_ref[0, h] = jnp.ones_like(m0)
            jump_ref[h] = jnp.zeros_like(m0)
            m_ref[h] = m0

    def stage_a(u, slot, heads=(0, 1)):
        for h in heads:
            st = qk(i - u, h)
            m_old = m_ref[h]
            pt = jnp.exp2(st - m_old)
            p_ref[slot, h] = pt.astype(BF16)
            cs_ref[slot, h] = jnp.sum(pt, axis=0, keepdims=True)
            tile_max = jnp.max(st, axis=0, keepdims=True)
            m_new = jnp.maximum(m_old, tile_max)
            beta_ref[slot, h] = jnp.exp2(m_old - m_new)
            jump_ref[h] = jnp.maximum(jump_ref[h], tile_max - m_old)
            m_ref[h] = m_new

    def stage_u(u, slot, heads=(0, 1)):
        for h in heads:
            beta = beta_ref[slot, h]
            acc_ref[h] = (acc_ref[h] + _dot(vt_ref[i - u], p_ref[slot, h])) * beta
            l_ref[h] = (l_ref[h] + cs_ref[slot, h]) * beta

    def full_tiles(u0, n):
        for k in range(n):
            for h in range(2):
                stage_a(u0 + k + 1, (k + 1) % 2, (h,))
                stage_u(u0 + k, k % 2, (h,))

    stage_a_diagonal()

    def eight_tiles(jj, carry):
        full_tiles(8 * jj, 8)
        return carry

    lax.fori_loop(0, i // 8, eight_tiles, 0)

    @pl.when(i % 8 >= 4)
    def _():
        full_tiles(8 * (i // 8), 4)

    @pl.when(i % 4 >= 2)
    def _():
        full_tiles(4 * (i // 4), 2)

    def finish():
        o0 = acc_ref[0] * (1.0 / l_ref[0])
        o1 = acc_ref[1] * (1.0 / l_ref[1])
        row = lax.broadcasted_iota(jnp.int32, o0.shape, 0)
        ot = jnp.where(row < HEAD_DIM, o0, o1)
        o_ref[...] = ot.T.astype(BF16)

    @pl.when(i % 2 == 1)
    def _():
        full_tiles(i - 1, 1)
        flag_ref[0] = jnp.max(jump_ref[...])
        stage_u(i, 1)
        finish()

    @pl.when(i % 2 == 0)
    def _():
        flag_ref[0] = jnp.max(jump_ref[...])
        stage_u(i, 0)
        finish()

    @pl.when(flag_ref[0] > MAX_JUMP)
    def _():
        m_ref[...] = jnp.full_like(m_ref, -jnp.inf)
        l_ref[...] = jnp.zeros_like(l_ref)
        acc_ref[...] = jnp.zeros_like(acc_ref)

        def two_pass(t, st, h):
            m_prev = m_ref[h]
            m_new = jnp.maximum(m_prev, jnp.max(st, axis=0, keepdims=True))
            alpha = jnp.exp2(m_prev - m_new)
            pt = jnp.exp2(st - m_new)
            l_ref[h] = alpha * l_ref[h] + jnp.sum(pt, axis=0, keepdims=True)
            acc_ref[h] = alpha * acc_ref[h] + _dot(vt_ref[t], pt.astype(BF16))
            m_ref[h] = m_new

        def full_tile(t, carry):
            for h in range(2):
                two_pass(t, qk(t, h), h)
            return carry

        lax.fori_loop(0, i, full_tile, 0)
        for h in range(2):
            two_pass(i, diagonal_scores(h), h)
        finish()


def _out_kernel(o_ref, x_ref, wo_ref, gmlp_ref, w1_ref, w2_ref, gfin_ref, out_ref):
    x1 = x_ref[...] + _dot(o_ref[...], wo_ref[...])
    h2 = _rms(x1, gmlp_ref[...]).astype(BF16)
    y = x1
    for c in range(w1_ref.shape[1] // FF_CHUNK):
        u = _dot(h2, w1_ref[:, c * FF_CHUNK:(c + 1) * FF_CHUNK])
        a = jnp.square(jnp.maximum(u, 0.0)).astype(BF16)
        y = y + _dot(a, w2_ref[c * FF_CHUNK:(c + 1) * FF_CHUNK, :])
    out_ref[...] = _rms(y, gfin_ref[...])


def _pad_heads(w, head_w):
    k = w.shape[0]
    w3 = w.reshape(k, N_HEADS, head_w)
    return jnp.pad(w3, ((0, 0), (0, 0), (0, LANES - head_w))).reshape(k, GROUP_W)


def _rope_tables(seq):
    half = ROPE_DIM // 2
    inv = ROPE_THETA ** (-np.arange(half, dtype=np.float64) / half)
    ang = np.arange(seq, dtype=np.float64)[:, None] * inv[None, :]
    cos, sin = np.cos(ang), np.sin(ang)
    cos_t = np.ones((seq, LANES), np.float64)
    sin_t = np.zeros((seq, LANES), np.float64)
    cos_t[:, HEAD_DIM:HEAD_DIM + half] = cos
    cos_t[:, HEAD_DIM + half:HEAD_DIM + ROPE_DIM] = cos
    sin_t[:, HEAD_DIM:HEAD_DIM + half] = -sin
    sin_t[:, HEAD_DIM + half:HEAD_DIM + ROPE_DIM] = sin
    return jnp.asarray(cos_t, F32), jnp.asarray(sin_t, F32)


def _placement():
    e = np.zeros((LANES, GROUP_W), np.float32)
    for t in range(3):
        for h in range(N_HEADS):
            e[t * TERM_LANES + h, h * LANES + BIAS_LANE + t] = 1.0
    qb = np.zeros((1, LANES), np.float32)
    qb[0, BIAS_LANE:BIAS_LANE + 3] = 1.0
    return jnp.asarray(e, BF16), jnp.asarray(qb, F32)


def _diagonal_masks():
    kpos = np.arange(ATT_TK)[:, None]
    qpos = np.arange(ATT_TQ)[None, :]
    visible = np.stack([kpos <= qpos, kpos // CHUNK <= qpos // CHUNK])
    return jnp.asarray(np.where(visible, 0.0, -np.inf), F32)


def _resident(shape):
    nd = len(shape)
    return pl.BlockSpec(shape, lambda *_: (0,) * nd, pipeline_mode=pl.Buffered(1))


def _layer(x2, g_mix, w_in, b_f, g_q, w_q_up, g_kv, w_kv_up, w_o, g_mlp, w_ff1, w_ff2, g_out):
    seq, d = x2.shape
    fw = FOX_W
    splits = np.cumsum([fw, fw, fw, N_HEADS, Q_LORA, KV_LORA])
    w_fq, w_fk, w_fv, w_lg, w_ql, w_kvl, w_kr = jnp.split(w_in, splits, axis=1)
    w_kr_g = jnp.pad(w_kr, ((0, 0), (HEAD_DIM, LANES - HEAD_DIM - ROPE_DIM)))
    w_gate = w_kr_g.at[:, 0:N_HEADS].set(w_lg)
    win = jnp.concatenate([w_fq, w_fk, w_gate, w_kvl, w_ql], axis=1).astype(BF16)
    wvt = w_fv.T.astype(BF16)
    bf = jnp.pad(b_f, (0, LANES - N_HEADS)).reshape(1, LANES)
    wqup = _pad_heads(w_q_up, HEAD_DIM + ROPE_DIM).astype(BF16)
    wkv3 = w_kv_up.reshape(KV_LORA, N_HEADS, 2 * HEAD_DIM)
    wkup = _pad_heads(wkv3[:, :, :HEAD_DIM].reshape(KV_LORA, fw), HEAD_DIM).astype(BF16)
    wvupt = wkv3[:, :, HEAD_DIM:].reshape(KV_LORA, fw).T.astype(BF16)
    cos_t, sin_t = _rope_tables(seq)
    tri = jnp.asarray(np.tril(np.ones((PROJ_TM, PROJ_TM), np.float32)), BF16)
    place, qbias = _placement()

    n_kv = seq // ATT_TK
    row = lambda i: (i, 0)
    q_all, k_all, vt_all = pl.pallas_call(
        _proj_kernel,
        grid=(seq // PROJ_TM,),
        in_specs=[
            pl.BlockSpec((PROJ_TM, d), row),
            _resident((1, d)), _resident(win.shape), _resident(wvt.shape), _resident((1, LANES)),
            _resident((1, Q_LORA)), _resident(wqup.shape), _resident((1, KV_LORA)), _resident(wkup.shape),
            _resident(wvupt.shape),
            pl.BlockSpec((PROJ_TM, LANES), row), pl.BlockSpec((PROJ_TM, LANES), row),
            _resident(tri.shape), _resident(place.shape), _resident(qbias.shape),
        ],
        out_specs=[
            pl.BlockSpec((PROJ_TM, 2 * GROUP_W), row),
            pl.BlockSpec((PROJ_TM, 2 * GROUP_W), row),
            pl.BlockSpec((PROJ_TM // ATT_TK, 2 * fw, ATT_TK), lambda i: (i, 0, 0)),
        ],
        out_shape=[
            jax.ShapeDtypeStruct((seq, 2 * GROUP_W), BF16),
            jax.ShapeDtypeStruct((seq, 2 * GROUP_W), BF16),
            jax.ShapeDtypeStruct((n_kv, 2 * fw, ATT_TK), BF16),
        ],
        scratch_shapes=[pltpu.VMEM((8, LANES), F32)],
        compiler_params=pltpu.CompilerParams(
            dimension_semantics=("arbitrary",), vmem_limit_bytes=48 * 1024 * 1024),
        name="proj",
    )(x2, g_mix.reshape(1, d), win, wvt, bf, g_q.reshape(1, Q_LORA), wqup, g_kv.reshape(1, KV_LORA),
      wkup, wvupt, cos_t, sin_t, tri, place, qbias)

    n_pairs = N_HEADS
    o_all = pl.pallas_call(
        _attn_kernel,
        grid=(n_pairs, seq // ATT_TQ),
        in_specs=[
            pl.BlockSpec((ATT_TQ, 2 * LANES), lambda p, i: (i, p)),
            pl.BlockSpec((seq, 2 * LANES), lambda p, i: (0, p)),
            pl.BlockSpec((n_kv, LANES, ATT_TK), lambda p, i: (0, p, 0)),
            pl.BlockSpec((1, ATT_TK, ATT_TQ), lambda p, i: (p // (N_HEADS // 2), 0, 0)),
        ],
        out_specs=pl.BlockSpec((ATT_TQ, LANES), lambda p, i: (i, p)),
        out_shape=jax.ShapeDtypeStruct((seq, 2 * fw), BF16),
        scratch_shapes=[
            pltpu.VMEM((2, 2, ATT_TK, ATT_TQ), BF16),
            pltpu.VMEM((2, 2, 1, ATT_TQ), F32),
            pltpu.VMEM((2, 2, 1, ATT_TQ), F32),
            pltpu.VMEM((2, LANES, ATT_TQ), F32),
            pltpu.VMEM((2, 1, ATT_TQ), F32),
            pltpu.VMEM((2, 1, ATT_TQ), F32),
            pltpu.VMEM((2, 1, ATT_TQ), F32),
            pltpu.SMEM((1,), F32),
        ],
        compiler_params=pltpu.CompilerParams(
            dimension_semantics=("arbitrary", "arbitrary"), vmem_limit_bytes=48 * 1024 * 1024),
        name="attn",
    )(q_all, k_all, vt_all, _diagonal_masks())

    d_ff = w_ff1.shape[1]
    return pl.pallas_call(
        _out_kernel,
        grid=(seq // OUT_TM,),
        in_specs=[
            pl.BlockSpec((OUT_TM, 2 * fw), row),
            pl.BlockSpec((OUT_TM, d), row),
            _resident((2 * fw, d)), _resident((1, d)), _resident((d, d_ff)), _resident((d_ff, d)),
            _resident((1, d)),
        ],
        out_specs=pl.BlockSpec((OUT_TM, d), row),
        out_shape=jax.ShapeDtypeStruct((seq, d), F32),
        compiler_params=pltpu.CompilerParams(
            dimension_semantics=("arbitrary",), vmem_limit_bytes=56 * 1024 * 1024),
        name="out_mlp",
    )(o_all, x2, w_o.astype(BF16), g_mlp.reshape(1, d), w_ff1.astype(BF16), w_ff2.astype(BF16),
      g_out.reshape(1, d))


def kernel(x, g_mix, w_in, b_f, g_q, w_q_up, g_kv, w_kv_up, w_o, g_mlp, w_ff1, w_ff2, g_final):
    b, seq, d = x.shape
    depth = w_in.shape[0]
    assert b == 1 and depth == 1, "single-sequence, single-layer problem"
    assert seq % PROJ_TM == 0 and seq % ATT_TQ == 0 and seq % OUT_TM == 0 and ATT_TQ % ATT_TK == 0
    out = _layer(x[0], g_mix[0], w_in[0], b_f[0], g_q[0], w_q_up[0], g_kv[0], w_kv_up[0], w_o[0],
                 g_mlp[0], w_ff1[0], w_ff2[0], g_final)
    return out[None]
```

```python
import math

import numpy as np
import jax
import jax.numpy as jnp
from jax import lax
from jax.experimental import pallas as pl
from jax.experimental.pallas import tpu as pltpu

F32 = jnp.float32
BF16 = jnp.bfloat16

EPS = 1e-6
CHUNK = 64
N_HEADS = 8
HEAD_DIM = 64
ROPE_DIM = 32
Q_LORA = 256
KV_LORA = 128
ROPE_THETA = 10000.0
LOG2E = math.log2(math.e)
FOX_SCALE = LOG2E / math.sqrt(HEAD_DIM)
MLA_SCALE = LOG2E / math.sqrt(HEAD_DIM + ROPE_DIM)

LANES = 128
MXU_COLS = 256
BIAS_LANE = HEAD_DIM
MAX_JUMP = 60.0

PROJ_TM = 512
ATT_TQ = 512
ATT_TK = 512
OUT_TM = 512
FF_CHUNK = 1024

GROUP_W = N_HEADS * LANES

FOX_W = N_HEADS * HEAD_DIM
_C_FQ = 0
_C_FK = _C_FQ + FOX_W
_C_GATE = _C_FK + FOX_W
_C_KVLAT = _C_GATE + LANES
_C_QLAT = _C_KVLAT + KV_LORA
TERM_LANES = N_HEADS


def _rms(x, g):
    return x * lax.rsqrt(jnp.mean(x * x, axis=-1, keepdims=True) + EPS) * g


def _split3_packed(a, lane):
    hi = a.astype(BF16).astype(F32)
    r = a - hi
    mid = r.astype(BF16).astype(F32)
    lo = r - mid
    packed = jnp.where(lane < TERM_LANES, hi,
                       jnp.where(lane < 2 * TERM_LANES, pltpu.roll(mid, TERM_LANES, 1),
                                 jnp.where(lane < 3 * TERM_LANES, pltpu.roll(lo, 2 * TERM_LANES, 1), 0.0)))
    return packed.astype(BF16)


def _sum3_packed(a):
    return a + pltpu.roll(a, LANES - TERM_LANES, 1) + pltpu.roll(a, LANES - 2 * TERM_LANES, 1)


def _rope(x, cos, sin, first_half):
    rot = jnp.where(first_half, pltpu.roll(x, LANES - ROPE_DIM // 2, 1), pltpu.roll(x, ROPE_DIM // 2, 1))
    return x * cos + rot * sin


def _spread_pair(pair, low, fill):
    return jnp.where(low, pair, fill), jnp.where(low, pltpu.roll(pair, HEAD_DIM, 1), fill)


def _dot(a, b):
    return jnp.dot(a, b, preferred_element_type=F32)


def _dot_nt(a, b):
    return lax.dot_general(a, b, (((1,), (1,)), ((), ())), preferred_element_type=F32)


def _proj_kernel(x_ref, gmix_ref, win_ref, wvt_ref, bf_ref, gq_ref, wqup_ref, gkv_ref, wkup_ref,
                 wvupt_ref, cos_ref, sin_ref, tri_ref, place_ref, qbias_ref,
                 q_out, k_out, vt_out, carry_ref):
    tm = x_ref.shape[0]

    @pl.when(pl.program_id(0) == 0)
    def _():
        carry_ref[...] = jnp.zeros_like(carry_ref)

    hb = _rms(x_ref[...], gmix_ref[...]).astype(BF16)

    lane = lax.broadcasted_iota(jnp.int32, (tm, LANES), 1)
    low = lane < HEAD_DIM

    for c in range(FOX_W // MXU_COLS):
        pq = _dot(hb, win_ref[:, _C_FQ + c * MXU_COLS:_C_FQ + (c + 1) * MXU_COLS]) * FOX_SCALE
        for jj in range(MXU_COLS // LANES):
            j = c * (MXU_COLS // LANES) + jj
            even, odd = _spread_pair(pq[:, jj * LANES:(jj + 1) * LANES], low, qbias_ref[...])
            q_out[:, 2 * j * LANES:(2 * j + 1) * LANES] = even.astype(BF16)
            q_out[:, (2 * j + 1) * LANES:(2 * j + 2) * LANES] = odd.astype(BF16)

    gk = _dot(hb, win_ref[:, _C_GATE:_C_GATE + LANES + KV_LORA])
    gate = gk[:, 0:LANES]
    z = gate + bf_ref[...]
    log_f = jnp.minimum(z, 0.0) - jnp.log1p(jnp.exp(-jnp.abs(z)))
    c = _sum3_packed(_dot(tri_ref[...], _split3_packed(log_f, lane))) + carry_ref[0:1, :]
    carry_ref[...] = jnp.broadcast_to(c[tm - 1:tm, :], carry_ref.shape)
    neg_c = _split3_packed(c * -LOG2E, lane)
    for cc in range(FOX_W // MXU_COLS):
        pk = _dot(hb, win_ref[:, _C_FK + cc * MXU_COLS:_C_FK + (cc + 1) * MXU_COLS])
        for jj in range(MXU_COLS // LANES):
            j = cc * (MXU_COLS // LANES) + jj
            even, odd = 2 * j * LANES, (2 * j + 1) * LANES
            placed = _dot(neg_c, place_ref[:, even:even + 2 * LANES])
            pair = pk[:, jj * LANES:(jj + 1) * LANES]
            k_out[:, even:even + LANES] = jnp.where(low, pair, placed[:, 0:LANES]).astype(BF16)
            k_out[:, odd:odd + LANES] = jnp.where(
                low, pltpu.roll(pair, HEAD_DIM, 1), placed[:, LANES:2 * LANES]).astype(BF16)

    cos = cos_ref[...]
    sin = sin_ref[...]
    first_half = lane < HEAD_DIM + ROPE_DIM // 2

    q_lat = _dot(hb, win_ref[:, _C_QLAT:_C_QLAT + Q_LORA])
    nq = _rms(q_lat, gq_ref[...]).astype(BF16)
    cos_q, sin_q = cos * MLA_SCALE, sin * MLA_SCALE
    for cc in range(GROUP_W // MXU_COLS):
        cq = _dot(nq, wqup_ref[:, cc * MXU_COLS:(cc + 1) * MXU_COLS])
        for hh in range(MXU_COLS // LANES):
            h = cc * (MXU_COLS // LANES) + hh
            xg = cq[:, hh * LANES:(hh + 1) * LANES]
            q_out[:, GROUP_W + h * LANES:GROUP_W + (h + 1) * LANES] = _rope(
                xg, cos_q, sin_q, first_half).astype(BF16)

    nkv = _rms(gk[:, LANES:LANES + KV_LORA], gkv_ref[...]).astype(BF16)
    kr = _rope(jnp.where(low, 0.0, gate), cos, sin, first_half)
    for cc in range(GROUP_W // MXU_COLS):
        kn = _dot(nkv, wkup_ref[:, cc * MXU_COLS:(cc + 1) * MXU_COLS])
        for hh in range(MXU_COLS // LANES):
            h = cc * (MXU_COLS // LANES) + hh
            k_out[:, GROUP_W + h * LANES:GROUP_W + (h + 1) * LANES] = (
                kn[:, hh * LANES:(hh + 1) * LANES] + kr).astype(BF16)

    vf = _dot_nt(wvt_ref[...], hb).astype(BF16)
    vm = _dot_nt(wvupt_ref[...], nkv).astype(BF16)
    nv = vf.shape[0]
    for b in range(tm // ATT_TK):
        vt_out[b, 0:nv, :] = vf[:, b * ATT_TK:(b + 1) * ATT_TK]
        vt_out[b, nv:2 * nv, :] = vm[:, b * ATT_TK:(b + 1) * ATT_TK]


def _attn_kernel(q_ref, qn_ref, k_ref, vt_ref, dmask_ref, o_ref, p_ref, cs_ref, beta_ref, acc_ref, m_ref, mn_ref,
                 l_ref, jump_ref, flag_ref):
    tq = q_ref.shape[0]
    tk = vt_ref.shape[2]
    assert tq == tk
    n_q = pl.num_programs(1)
    i = pl.program_id(1)
    DIAG = 2

    def qk(t, h):
        kh = k_ref[pl.ds(pl.multiple_of(t * tk, tk), tk), h * LANES:(h + 1) * LANES]
        return _dot_nt(kh, q_ref[:, h * LANES:(h + 1) * LANES])

    def diagonal_scores(h):
        return qk(i, h) + dmask_ref[0]

    def stage_a_diagonal(t, queries):
        hk, hq = tk // 2, tq // 2
        assert hk % CHUNK == 0 and hk == hq
        for h in range(2):
            qh = queries[:, h * LANES:(h + 1) * LANES]
            k0 = pl.multiple_of(t * tk, tk)
            st_top = _dot_nt(k_ref[pl.ds(k0, hk), h * LANES:(h + 1) * LANES], qh) + dmask_ref[0, 0:hk, :]
            st_bot = (_dot_nt(k_ref[pl.ds(k0 + hk, hk), h * LANES:(h + 1) * LANES], qh[hq:, :])
                      + dmask_ref[0, hk:, hq:])
            m_early = jnp.max(st_top[:, :hq], axis=0, keepdims=True)
            m_late = jnp.maximum(jnp.max(st_top[:, hq:], axis=0, keepdims=True),
                                 jnp.max(st_bot, axis=0, keepdims=True))
            m0 = jnp.concatenate([m_early, m_late], axis=1)
            p_top = jnp.exp2(st_top - m0)
            p_bot = jnp.exp2(st_bot - m_late)
            p_ref[DIAG, h, 0:hk, :] = p_top.astype(BF16)
            p_ref[DIAG, h, hk:, 0:hq] = jnp.zeros((hk, hq), BF16)
            p_ref[DIAG, h, hk:, hq:] = p_bot.astype(BF16)
            cs_ref[DIAG, h] = jnp.concatenate(
                [jnp.sum(p_top[:, :hq], axis=0, keepdims=True),
                 jnp.sum(p_top[:, hq:], axis=0, keepdims=True) + jnp.sum(p_bot, axis=0, keepdims=True)], axis=1)
            mn_ref[h] = m0

    def stage_u_diagonal():
        for h in range(2):
            acc_ref[h] = _dot(vt_ref[i], p_ref[DIAG, h])
            l_ref[h] = cs_ref[DIAG, h]

    def stage_a(u, slot, heads=(0, 1)):
        for h in heads:
            st = qk(i - u, h)
            m_old = m_ref[h]
            pt = jnp.exp2(st - m_old)
            p_ref[slot, h] = pt.astype(BF16)
            cs_ref[slot, h] = jnp.sum(pt, axis=0, keepdims=True)
            tile_max = jnp.max(st, axis=0, keepdims=True)
            m_new = jnp.maximum(m_old, tile_max)
            beta_ref[slot, h] = jnp.exp2(m_old - m_new)
            jump_ref[h] = jnp.maximum(jump_ref[h], tile_max - m_old)
            m_ref[h] = m_new

    def stage_u(u, slot, heads=(0, 1)):
        for h in heads:
            beta = beta_ref[slot, h]
            acc_ref[h] = (acc_ref[h] + _dot(vt_ref[i - u], p_ref[slot, h])) * beta
            l_ref[h] = (l_ref[h] + cs_ref[slot, h]) * beta

    def full_tiles(u0, n):
        for k in range(n):
            for h in range(2):
                stage_a(u0 + k + 1, k % 2, (h,))
                stage_u(u0 + k, (k + 1) % 2, (h,))

    def finish():
        o0 = acc_ref[0] * (1.0 / l_ref[0])
        o1 = acc_ref[1] * (1.0 / l_ref[1])
        row = lax.broadcasted_iota(jnp.int32, o0.shape, 0)
        ot = jnp.where(row < HEAD_DIM, o0, o1)
        o_ref[...] = ot.T.astype(BF16)

    def next_diagonal():
        stage_a_diagonal(jnp.minimum(i + 1, n_q - 1), qn_ref)

    @pl.when(i == 0)
    def _():
        stage_a_diagonal(i, q_ref)

    m_ref[...] = mn_ref[...]
    jump_ref[...] = jnp.zeros_like(jump_ref)

    @pl.when(i > 0)
    def _():
        stage_a(1, 1)
        stage_u_diagonal()

    n_ahead = jnp.maximum(i - 1, 0)

    def eight_tiles(jj, carry):
        full_tiles(1 + 8 * jj, 8)
        return carry

    lax.fori_loop(0, n_ahead // 8, eight_tiles, 0)

    @pl.when(n_ahead % 8 >= 4)
    def _():
        full_tiles(1 + 8 * (n_ahead // 8), 4)

    @pl.when(n_ahead % 4 >= 2)
    def _():
        full_tiles(1 + 4 * (n_ahead // 4), 2)

    @pl.when(i == 0)
    def _():
        flag_ref[0] = jnp.max(jump_ref[...])
        stage_u_diagonal()
        next_diagonal()
        finish()

    @pl.when(i % 2 == 1)
    def _():
        flag_ref[0] = jnp.max(jump_ref[...])
        stage_u(i, 1)
        next_diagonal()
        finish()

    @pl.when(jnp.logical_and(i > 0, i % 2 == 0))
    def _():
        full_tiles(i - 1, 1)
        flag_ref[0] = jnp.max(jump_ref[...])
        stage_u(i, 0)
        next_diagonal()
        finish()

    @pl.when(flag_ref[0] > MAX_JUMP)
    def _():
        m_ref[...] = jnp.full_like(m_ref, -jnp.inf)
        l_ref[...] = jnp.zeros_like(l_ref)
        acc_ref[...] = jnp.zeros_like(acc_ref)

        def two_pass(t, st, h):
            m_prev = m_ref[h]
            m_new = jnp.maximum(m_prev, jnp.max(st, axis=0, keepdims=True))
            alpha = jnp.exp2(m_prev - m_new)
            pt = jnp.exp2(st - m_new)
            l_ref[h] = alpha * l_ref[h] + jnp.sum(pt, axis=0, keepdims=True)
            acc_ref[h] = alpha * acc_ref[h] + _dot(vt_ref[t], pt.astype(BF16))
            m_ref[h] = m_new

        def full_tile(t, carry):
            for h in range(2):
                two_pass(t, qk(t, h), h)
            return carry

        lax.fori_loop(0, i, full_tile, 0)
        for h in range(2):
            two_pass(i, diagonal_scores(h), h)
        finish()


def _out_kernel(o_ref, x_ref, wo_ref, gmlp_ref, w1_ref, w2_ref, gfin_ref, out_ref):
    x1 = x_ref[...] + _dot(o_ref[...], wo_ref[...])
    h2 = _rms(x1, gmlp_ref[...]).astype(BF16)
    y = x1
    for c in range(w1_ref.shape[1] // FF_CHUNK):
        u = _dot(h2, w1_ref[:, c * FF_CHUNK:(c + 1) * FF_CHUNK])
        a = jnp.square(jnp.maximum(u, 0.0)).astype(BF16)
        y = y + _dot(a, w2_ref[c * FF_CHUNK:(c + 1) * FF_CHUNK, :])
    out_ref[...] = _rms(y, gfin_ref[...])


def _pad_heads(w, head_w):
    k = w.shape[0]
    w3 = w.reshape(k, N_HEADS, head_w)
    return jnp.pad(w3, ((0, 0), (0, 0), (0, LANES - head_w))).reshape(k, GROUP_W)


def _rope_tables(seq):
    half = ROPE_DIM // 2
    inv = ROPE_THETA ** (-np.arange(half, dtype=np.float64) / half)
    ang = np.arange(seq, dtype=np.float64)[:, None] * inv[None, :]
    cos, sin = np.cos(ang), np.sin(ang)
    cos_t = np.ones((seq, LANES), np.float64)
    sin_t = np.zeros((seq, LANES), np.float64)
    cos_t[:, HEAD_DIM:HEAD_DIM + half] = cos
    cos_t[:, HEAD_DIM + half:HEAD_DIM + ROPE_DIM] = cos
    sin_t[:, HEAD_DIM:HEAD_DIM + half] = -sin
    sin_t[:, HEAD_DIM + half:HEAD_DIM + ROPE_DIM] = sin
    return jnp.asarray(cos_t, F32), jnp.asarray(sin_t, F32)


def _placement():
    e = np.zeros((LANES, GROUP_W), np.float32)
    for t in range(3):
        for h in range(N_HEADS):
            e[t * TERM_LANES + h, h * LANES + BIAS_LANE + t] = 1.0
    qb = np.zeros((1, LANES), np.float32)
    qb[0, BIAS_LANE:BIAS_LANE + 3] = 1.0
    return jnp.asarray(e, BF16), jnp.asarray(qb, F32)


def _diagonal_masks():
    kpos = np.arange(ATT_TK)[:, None]
    qpos = np.arange(ATT_TQ)[None, :]
    visible = np.stack([kpos <= qpos, kpos // CHUNK <= qpos // CHUNK])
    return jnp.asarray(np.where(visible, 0.0, -np.inf), F32)


def _resident(shape):
    nd = len(shape)
    return pl.BlockSpec(shape, lambda *_: (0,) * nd, pipeline_mode=pl.Buffered(1))


def _layer(x2, g_mix, w_in, b_f, g_q, w_q_up, g_kv, w_kv_up, w_o, g_mlp, w_ff1, w_ff2, g_out):
    seq, d = x2.shape
    fw = FOX_W
    splits = np.cumsum([fw, fw, fw, N_HEADS, Q_LORA, KV_LORA])
    w_fq, w_fk, w_fv, w_lg, w_ql, w_kvl, w_kr = jnp.split(w_in, splits, axis=1)
    w_kr_g = jnp.pad(w_kr, ((0, 0), (HEAD_DIM, LANES - HEAD_DIM - ROPE_DIM)))
    w_gate = w_kr_g.at[:, 0:N_HEADS].set(w_lg)
    win = jnp.concatenate([w_fq, w_fk, w_gate, w_kvl, w_ql], axis=1).astype(BF16)
    wvt = w_fv.T.astype(BF16)
    bf = jnp.pad(b_f, (0, LANES - N_HEADS)).reshape(1, LANES)
    wqup = _pad_heads(w_q_up, HEAD_DIM + ROPE_DIM).astype(BF16)
    wkv3 = w_kv_up.reshape(KV_LORA, N_HEADS, 2 * HEAD_DIM)
    wkup = _pad_heads(wkv3[:, :, :HEAD_DIM].reshape(KV_LORA, fw), HEAD_DIM).astype(BF16)
    wvupt = wkv3[:, :, HEAD_DIM:].reshape(KV_LORA, fw).T.astype(BF16)
    cos_t, sin_t = _rope_tables(seq)
    tri = jnp.asarray(np.tril(np.ones((PROJ_TM, PROJ_TM), np.float32)), BF16)
    place, qbias = _placement()

    n_kv = seq // ATT_TK
    row = lambda i: (i, 0)
    q_all, k_all, vt_all = pl.pallas_call(
        _proj_kernel,
        grid=(seq // PROJ_TM,),
        in_specs=[
            pl.BlockSpec((PROJ_TM, d), row),
            _resident((1, d)), _resident(win.shape), _resident(wvt.shape), _resident((1, LANES)),
            _resident((1, Q_LORA)), _resident(wqup.shape), _resident((1, KV_LORA)), _resident(wkup.shape),
            _resident(wvupt.shape),
            pl.BlockSpec((PROJ_TM, LANES), row), pl.BlockSpec((PROJ_TM, LANES), row),
            _resident(tri.shape), _resident(place.shape), _resident(qbias.shape),
        ],
        out_specs=[
            pl.BlockSpec((PROJ_TM, 2 * GROUP_W), row),
            pl.BlockSpec((PROJ_TM, 2 * GROUP_W), row),
            pl.BlockSpec((PROJ_TM // ATT_TK, 2 * fw, ATT_TK), lambda i: (i, 0, 0)),
        ],
        out_shape=[
            jax.ShapeDtypeStruct((seq, 2 * GROUP_W), BF16),
            jax.ShapeDtypeStruct((seq, 2 * GROUP_W), BF16),
            jax.ShapeDtypeStruct((n_kv, 2 * fw, ATT_TK), BF16),
        ],
        scratch_shapes=[pltpu.VMEM((8, LANES), F32)],
        compiler_params=pltpu.CompilerParams(
            dimension_semantics=("arbitrary",), vmem_limit_bytes=48 * 1024 * 1024),
        name="proj",
    )(x2, g_mix.reshape(1, d), win, wvt, bf, g_q.reshape(1, Q_LORA), wqup, g_kv.reshape(1, KV_LORA),
      wkup, wvupt, cos_t, sin_t, tri, place, qbias)

    n_pairs = N_HEADS
    o_all = pl.pallas_call(
        _attn_kernel,
        grid=(n_pairs, seq // ATT_TQ),
        in_specs=[
            pl.BlockSpec((ATT_TQ, 2 * LANES), lambda p, i: (i, p)),
            pl.BlockSpec((ATT_TQ, 2 * LANES), lambda p, i: (jnp.minimum(i + 1, seq // ATT_TQ - 1), p)),
            pl.BlockSpec((seq, 2 * LANES), lambda p, i: (0, p)),
            pl.BlockSpec((n_kv, LANES, ATT_TK), lambda p, i: (0, p, 0)),
            pl.BlockSpec((1, ATT_TK, ATT_TQ), lambda p, i: (p // (N_HEADS // 2), 0, 0)),
        ],
        out_specs=pl.BlockSpec((ATT_TQ, LANES), lambda p, i: (i, p)),
        out_shape=jax.ShapeDtypeStruct((seq, 2 * fw), BF16),
        scratch_shapes=[
            pltpu.VMEM((3, 2, ATT_TK, ATT_TQ), BF16),
            pltpu.VMEM((3, 2, 1, ATT_TQ), F32),
            pltpu.VMEM((2, 2, 1, ATT_TQ), F32),
            pltpu.VMEM((2, LANES, ATT_TQ), F32),
            pltpu.VMEM((2, 1, ATT_TQ), F32),
            pltpu.VMEM((2, 1, ATT_TQ), F32),
            pltpu.VMEM((2, 1, ATT_TQ), F32),
            pltpu.VMEM((2, 1, ATT_TQ), F32),
            pltpu.SMEM((1,), F32),
        ],
        compiler_params=pltpu.CompilerParams(
            dimension_semantics=("arbitrary", "arbitrary"), vmem_limit_bytes=48 * 1024 * 1024),
        name="attn",
    )(q_all, q_all, k_all, vt_all, _diagonal_masks())

    d_ff = w_ff1.shape[1]
    return pl.pallas_call(
        _out_kernel,
        grid=(seq // OUT_TM,),
        in_specs=[
            pl.BlockSpec((OUT_TM, 2 * fw), row),
            pl.BlockSpec((OUT_TM, d), row),
            _resident((2 * fw, d)), _resident((1, d)), _resident((d, d_ff)), _resident((d_ff, d)),
            _resident((1, d)),
        ],
        out_specs=pl.BlockSpec((OUT_TM, d), row),
        out_shape=jax.ShapeDtypeStruct((seq, d), F32),
        compiler_params=pltpu.CompilerParams(
            dimension_semantics=("arbitrary",), vmem_limit_bytes=56 * 1024 * 1024),
        name="out_mlp",
    )(o_all, x2, w_o.astype(BF16), g_mlp.reshape(1, d), w_ff1.astype(BF16), w_ff2.astype(BF16),
      g_out.reshape(1, d))


def kernel(x, g_mix, w_in, b_f, g_q, w_q_up, g_kv, w_kv_up, w_o, g_mlp, w_ff1, w_ff2, g_final):
    b, seq, d = x.shape
    depth = w_in.shape[0]
    assert b == 1 and depth == 1, "single-sequence, single-layer problem"
    assert seq % PROJ_TM == 0 and seq % ATT_TQ == 0 and seq % OUT_TM == 0 and ATT_TQ % ATT_TK == 0
    out = _layer(x[0], g_mix[0], w_in[0], b_f[0], g_q[0], w_q_up[0], g_kv[0], w_kv_up[0], w_o[0],
                 g_mlp[0], w_ff1[0], w_ff2[0], g_final)
    return out[None]
```

```python
import math

import numpy as np
import jax
import jax.numpy as jnp
from jax import lax
from jax.experimental import pallas as pl
from jax.experimental.pallas import tpu as pltpu

F32 = jnp.float32
BF16 = jnp.bfloat16

EPS = 1e-6
CHUNK = 64
N_HEADS = 8
HEAD_DIM = 64
ROPE_DIM = 32
Q_LORA = 256
KV_LORA = 128
ROPE_THETA = 10000.0
LOG2E = math.log2(math.e)
FOX_SCALE = LOG2E / math.sqrt(HEAD_DIM)
MLA_SCALE = LOG2E / math.sqrt(HEAD_DIM + ROPE_DIM)

LANES = 128
BIAS_LANE = HEAD_DIM
MAX_JUMP = 60.0

PROJ_TM = 512
ATT_TQ = 512
ATT_TK = 512
OUT_TM = 512
FF_CHUNK = 1024

GROUP_W = N_HEADS * LANES

FOX_W = N_HEADS * HEAD_DIM
_C_FQ = 0
_C_FK = _C_FQ + FOX_W
_C_GATE = _C_FK + FOX_W
_C_KVLAT = _C_GATE + LANES
_C_QLAT = _C_KVLAT + KV_LORA
TERM_LANES = N_HEADS


def _rms(x, g):
    return x * lax.rsqrt(jnp.mean(x * x, axis=-1, keepdims=True) + EPS) * g


def _split3_packed(a, lane):
    hi = a.astype(BF16).astype(F32)
    r = a - hi
    mid = r.astype(BF16).astype(F32)
    lo = r - mid
    packed = jnp.where(lane < TERM_LANES, hi,
                       jnp.where(lane < 2 * TERM_LANES, pltpu.roll(mid, TERM_LANES, 1),
                                 jnp.where(lane < 3 * TERM_LANES, pltpu.roll(lo, 2 * TERM_LANES, 1), 0.0)))
    return packed.astype(BF16)


def _sum3_packed(a):
    return a + pltpu.roll(a, LANES - TERM_LANES, 1) + pltpu.roll(a, LANES - 2 * TERM_LANES, 1)


def _rope(x, cos, sin, first_half):
    rot = jnp.where(first_half, pltpu.roll(x, LANES - ROPE_DIM // 2, 1), pltpu.roll(x, ROPE_DIM // 2, 1))
    return x * cos + rot * sin


def _spread_pair(pair, low, fill):
    return jnp.where(low, pair, fill), jnp.where(low, pltpu.roll(pair, HEAD_DIM, 1), fill)


def _dot(a, b):
    return jnp.dot(a, b, preferred_element_type=F32)


def _dot_nt(a, b):
    return lax.dot_general(a, b, (((1,), (1,)), ((), ())), preferred_element_type=F32)


def _proj_kernel(x_ref, gmix_ref, win_ref, wvt_ref, bf_ref, gq_ref, wqup_ref, gkv_ref, wkup_ref,
                 wvupt_ref, cos_ref, sin_ref, tri_ref, place_ref, qbias_ref, wo_ref, w1_ref, w2_ref,
                 q_out, k_out, vt_out, wo_out, w1_out, w2_out, carry_ref):
    tm = x_ref.shape[0]

    wo_out[...] = wo_ref[...].astype(BF16)
    w1_out[...] = w1_ref[...].astype(BF16)
    w2_out[...] = w2_ref[...].astype(BF16)

    @pl.when(pl.program_id(0) == 0)
    def _():
        carry_ref[...] = jnp.zeros_like(carry_ref)

    hb = _rms(x_ref[...], gmix_ref[...]).astype(BF16)

    lane = lax.broadcasted_iota(jnp.int32, (tm, LANES), 1)
    low = lane < HEAD_DIM

    pq = _dot(hb, win_ref[:, _C_FQ:_C_FQ + FOX_W]) * FOX_SCALE
    for j in range(N_HEADS // 2):
        even, odd = _spread_pair(pq[:, j * LANES:(j + 1) * LANES], low, qbias_ref[...])
        q_out[:, 2 * j * LANES:(2 * j + 1) * LANES] = even.astype(BF16)
        q_out[:, (2 * j + 1) * LANES:(2 * j + 2) * LANES] = odd.astype(BF16)

    gk = _dot(hb, win_ref[:, _C_GATE:_C_GATE + LANES + KV_LORA])
    gate = gk[:, 0:LANES]
    z = gate + bf_ref[...]
    log_f = jnp.minimum(z, 0.0) - jnp.log1p(jnp.exp(-jnp.abs(z)))
    c = _sum3_packed(_dot(tri_ref[...], _split3_packed(log_f, lane))) + carry_ref[0:1, :]
    carry_ref[...] = jnp.broadcast_to(c[tm - 1:tm, :], carry_ref.shape)
    placed = _dot(_split3_packed(c * -LOG2E, lane), place_ref[...])
    pk = _dot(hb, win_ref[:, _C_FK:_C_FK + FOX_W])
    for j in range(N_HEADS // 2):
        even, odd = 2 * j * LANES, (2 * j + 1) * LANES
        pair = pk[:, j * LANES:(j + 1) * LANES]
        k_out[:, even:even + LANES] = jnp.where(low, pair, placed[:, even:even + LANES]).astype(BF16)
        k_out[:, odd:odd + LANES] = jnp.where(
            low, pltpu.roll(pair, HEAD_DIM, 1), placed[:, odd:odd + LANES]).astype(BF16)

    cos = cos_ref[...]
    sin = sin_ref[...]
    first_half = lane < HEAD_DIM + ROPE_DIM // 2

    q_lat = _dot(hb, win_ref[:, _C_QLAT:_C_QLAT + Q_LORA])
    nq = _rms(q_lat, gq_ref[...]).astype(BF16)
    cq = _dot(nq, wqup_ref[...])
    cos_q, sin_q = cos * MLA_SCALE, sin * MLA_SCALE
    for h in range(N_HEADS):
        xg = cq[:, h * LANES:(h + 1) * LANES]
        q_out[:, GROUP_W + h * LANES:GROUP_W + (h + 1) * LANES] = _rope(xg, cos_q, sin_q, first_half).astype(BF16)

    nkv = _rms(gk[:, LANES:LANES + KV_LORA], gkv_ref[...]).astype(BF16)
    kn = _dot(nkv, wkup_ref[...])
    kr = _rope(jnp.where(low, 0.0, gate), cos, sin, first_half)
    for h in range(N_HEADS):
        k_out[:, GROUP_W + h * LANES:GROUP_W + (h + 1) * LANES] = (kn[:, h * LANES:(h + 1) * LANES] + kr).astype(BF16)

    vf = _dot_nt(wvt_ref[...], hb).astype(BF16)
    vm = _dot_nt(wvupt_ref[...], nkv).astype(BF16)
    nv = vf.shape[0]
    for b in range(tm // ATT_TK):
        vt_out[b, 0:nv, :] = vf[:, b * ATT_TK:(b + 1) * ATT_TK]
        vt_out[b, nv:2 * nv, :] = vm[:, b * ATT_TK:(b + 1) * ATT_TK]


def _attn_kernel(q_ref, k_ref, vt_ref, dmask_ref, o_ref, p_ref, cs_ref, beta_ref, acc_ref, m_ref, l_ref, jump_ref,
                 flag_ref):
    tq = q_ref.shape[0]
    tk = vt_ref.shape[2]
    assert tq == tk
    i = pl.program_id(1)

    def qk(t, h):
        kh = k_ref[pl.ds(pl.multiple_of(t * tk, tk), tk), h * LANES:(h + 1) * LANES]
        return _dot_nt(kh, q_ref[:, h * LANES:(h + 1) * LANES])

    def diagonal_scores(h):
        return qk(i, h) + dmask_ref[0]

    l_ref[...] = jnp.zeros_like(l_ref)
    acc_ref[...] = jnp.zeros_like(acc_ref)

    def stage_a_diagonal():
        hk, hq = tk // 2, tq // 2
        assert hk % CHUNK == 0 and hk == hq
        for h in range(2):
            qh = q_ref[:, h * LANES:(h + 1) * LANES]
            k0 = pl.multiple_of(i * tk, tk)
            st_top = _dot_nt(k_ref[pl.ds(k0, hk), h * LANES:(h + 1) * LANES], qh) + dmask_ref[0, 0:hk, :]
            st_bot = (_dot_nt(k_ref[pl.ds(k0 + hk, hk), h * LANES:(h + 1) * LANES], qh[hq:, :])
                      + dmask_ref[0, hk:, hq:])
            m_early = jnp.max(st_top[:, :hq], axis=0, keepdims=True)
            m_late = jnp.maximum(jnp.max(st_top[:, hq:], axis=0, keepdims=True),
                                 jnp.max(st_bot, axis=0, keepdims=True))
            m0 = jnp.concatenate([m_early, m_late], axis=1)
            p_top = jnp.exp2(st_top - m0)
            p_bot = jnp.exp2(st_bot - m_late)
            p_ref[0, h, 0:hk, :] = p_top.astype(BF16)
            p_ref[0, h, hk:, 0:hq] = jnp.zeros((hk, hq), BF16)
            p_ref[0, h, hk:, hq:] = p_bot.astype(BF16)
            cs_ref[0, h] = jnp.concatenate(
                [jnp.sum(p_top[:, :hq], axis=0, keepdims=True),
                 jnp.sum(p_top[:, hq:], axis=0, keepdims=True) + jnp.sum(p_bot, axis=0, keepdims=True)], axis=1)
            beta_ref[0, h] = jnp.ones_like(m0)
            jump_ref[h] = jnp.zeros_like(m0)
            m_ref[h] = m0

    def stage_a(u, slot, heads=(0, 1)):
        for h in heads:
            st = qk(i - u, h)
            m_old = m_ref[h]
            pt = jnp.exp2(st - m_old)
            p_ref[slot, h] = pt.astype(BF16)
            cs_ref[slot, h] = jnp.sum(pt, axis=0, keepdims=True)
            tile_max = jnp.max(st, axis=0, keepdims=True)
            m_new = jnp.maximum(m_old, tile_max)
            beta_ref[slot, h] = jnp.exp2(m_old - m_new)
            jump_ref[h] = jnp.maximum(jump_ref[h], tile_max - m_old)
            m_ref[h] = m_new

    def stage_u(u, slot, heads=(0, 1)):
        for h in heads:
            beta = beta_ref[slot, h]
            acc_ref[h] = (acc_ref[h] + _dot(vt_ref[i - u], p_ref[slot, h])) * beta
            l_ref[h] = (l_ref[h] + cs_ref[slot, h]) * beta

    def full_tiles(u0, n):
        for k in range(n):
            for h in range(2):
                stage_a(u0 + k + 1, (k + 1) % 2, (h,))
                stage_u(u0 + k, k % 2, (h,))

    stage_a_diagonal()

    def eight_tiles(jj, carry):
        full_tiles(8 * jj, 8)
        return carry

    lax.fori_loop(0, i // 8, eight_tiles, 0)

    @pl.when(i % 8 >= 4)
    def _():
        full_tiles(8 * (i // 8), 4)

    @pl.when(i % 4 >= 2)
    def _():
        full_tiles(4 * (i // 4), 2)

    def finish():
        o0 = acc_ref[0] * (1.0 / l_ref[0])
        o1 = acc_ref[1] * (1.0 / l_ref[1])
        row = lax.broadcasted_iota(jnp.int32, o0.shape, 0)
        ot = jnp.where(row < HEAD_DIM, o0, o1)
        o_ref[...] = ot.T.astype(BF16)

    @pl.when(i % 2 == 1)
    def _():
        full_tiles(i - 1, 1)
        flag_ref[0] = jnp.max(jump_ref[...])
        stage_u(i, 1)
        finish()

    @pl.when(i % 2 == 0)
    def _():
        flag_ref[0] = jnp.max(jump_ref[...])
        stage_u(i, 0)
        finish()

    @pl.when(flag_ref[0] > MAX_JUMP)
    def _():
        m_ref[...] = jnp.full_like(m_ref, -jnp.inf)
        l_ref[...] = jnp.zeros_like(l_ref)
        acc_ref[...] = jnp.zeros_like(acc_ref)

        def two_pass(t, st, h):
            m_prev = m_ref[h]
            m_new = jnp.maximum(m_prev, jnp.max(st, axis=0, keepdims=True))
            alpha = jnp.exp2(m_prev - m_new)
            pt = jnp.exp2(st - m_new)
            l_ref[h] = alpha * l_ref[h] + jnp.sum(pt, axis=0, keepdims=True)
            acc_ref[h] = alpha * acc_ref[h] + _dot(vt_ref[t], pt.astype(BF16))
            m_ref[h] = m_new

        def full_tile(t, carry):
            for h in range(2):
                two_pass(t, qk(t, h), h)
            return carry

        lax.fori_loop(0, i, full_tile, 0)
        for h in range(2):
            two_pass(i, diagonal_scores(h), h)
        finish()


def _out_kernel(o_ref, x_ref, wo_ref, gmlp_ref, w1_ref, w2_ref, gfin_ref, out_ref):
    x1 = x_ref[...] + _dot(o_ref[...], wo_ref[...])
    h2 = _rms(x1, gmlp_ref[...]).astype(BF16)
    y = x1
    for c in range(w1_ref.shape[1] // FF_CHUNK):
        u = _dot(h2, w1_ref[:, c * FF_CHUNK:(c + 1) * FF_CHUNK])
        a = jnp.square(jnp.maximum(u, 0.0)).astype(BF16)
        y = y + _dot(a, w2_ref[c * FF_CHUNK:(c + 1) * FF_CHUNK, :])
    out_ref[...] = _rms(y, gfin_ref[...])


def _pad_heads(w, head_w):
    k = w.shape[0]
    w3 = w.reshape(k, N_HEADS, head_w)
    return jnp.pad(w3, ((0, 0), (0, 0), (0, LANES - head_w))).reshape(k, GROUP_W)


def _rope_tables(seq):
    half = ROPE_DIM // 2
    inv = ROPE_THETA ** (-np.arange(half, dtype=np.float64) / half)
    ang = np.arange(seq, dtype=np.float64)[:, None] * inv[None, :]
    cos, sin = np.cos(ang), np.sin(ang)
    cos_t = np.ones((seq, LANES), np.float64)
    sin_t = np.zeros((seq, LANES), np.float64)
    cos_t[:, HEAD_DIM:HEAD_DIM + half] = cos
    cos_t[:, HEAD_DIM + half:HEAD_DIM + ROPE_DIM] = cos
    sin_t[:, HEAD_DIM:HEAD_DIM + half] = -sin
    sin_t[:, HEAD_DIM + half:HEAD_DIM + ROPE_DIM] = sin
    return jnp.asarray(cos_t, F32), jnp.asarray(sin_t, F32)


def _placement():
    e = np.zeros((LANES, GROUP_W), np.float32)
    for t in range(3):
        for h in range(N_HEADS):
            e[t * TERM_LANES + h, h * LANES + BIAS_LANE + t] = 1.0
    qb = np.zeros((1, LANES), np.float32)
    qb[0, BIAS_LANE:BIAS_LANE + 3] = 1.0
    return jnp.asarray(e, BF16), jnp.asarray(qb, F32)


def _diagonal_masks():
    kpos = np.arange(ATT_TK)[:, None]
    qpos = np.arange(ATT_TQ)[None, :]
    visible = np.stack([kpos <= qpos, kpos // CHUNK <= qpos // CHUNK])
    return jnp.asarray(np.where(visible, 0.0, -np.inf), F32)


def _resident(shape):
    nd = len(shape)
    return pl.BlockSpec(shape, lambda *_: (0,) * nd, pipeline_mode=pl.Buffered(1))


def _layer(x2, g_mix, w_in, b_f, g_q, w_q_up, g_kv, w_kv_up, w_o, g_mlp, w_ff1, w_ff2, g_out):
    seq, d = x2.shape
    fw = FOX_W
    splits = np.cumsum([fw, fw, fw, N_HEADS, Q_LORA, KV_LORA])
    w_fq, w_fk, w_fv, w_lg, w_ql, w_kvl, w_kr = jnp.split(w_in, splits, axis=1)
    w_kr_g = jnp.pad(w_kr, ((0, 0), (HEAD_DIM, LANES - HEAD_DIM - ROPE_DIM)))
    w_gate = w_kr_g.at[:, 0:N_HEADS].set(w_lg)
    win = jnp.concatenate([w_fq, w_fk, w_gate, w_kvl, w_ql], axis=1).astype(BF16)
    wvt = w_fv.T.astype(BF16)
    bf = jnp.pad(b_f, (0, LANES - N_HEADS)).reshape(1, LANES)
    wqup = _pad_heads(w_q_up, HEAD_DIM + ROPE_DIM).astype(BF16)
    wkv3 = w_kv_up.reshape(KV_LORA, N_HEADS, 2 * HEAD_DIM)
    wkup = _pad_heads(wkv3[:, :, :HEAD_DIM].reshape(KV_LORA, fw), HEAD_DIM).astype(BF16)
    wvupt = wkv3[:, :, HEAD_DIM:].reshape(KV_LORA, fw).T.astype(BF16)
    cos_t, sin_t = _rope_tables(seq)
    tri = jnp.asarray(np.tril(np.ones((PROJ_TM, PROJ_TM), np.float32)), BF16)
    place, qbias = _placement()

    n_kv = seq // ATT_TK
    row = lambda i: (i, 0)
    n_steps = seq // PROJ_TM
    d_ff = w_ff1.shape[1]
    slab = lambda w: pl.BlockSpec((w.shape[0] // n_steps, w.shape[1]), row)
    q_all, k_all, vt_all, wo_b, w1_b, w2_b = pl.pallas_call(
        _proj_kernel,
        grid=(seq // PROJ_TM,),
        in_specs=[
            pl.BlockSpec((PROJ_TM, d), row),
            _resident((1, d)), _resident(win.shape), _resident(wvt.shape), _resident((1, LANES)),
            _resident((1, Q_LORA)), _resident(wqup.shape), _resident((1, KV_LORA)), _resident(wkup.shape),
            _resident(wvupt.shape),
            pl.BlockSpec((PROJ_TM, LANES), row), pl.BlockSpec((PROJ_TM, LANES), row),
            _resident(tri.shape), _resident(place.shape), _resident(qbias.shape),
            slab(w_o), slab(w_ff1), slab(w_ff2),
        ],
        out_specs=[
            pl.BlockSpec((PROJ_TM, 2 * GROUP_W), row),
            pl.BlockSpec((PROJ_TM, 2 * GROUP_W), row),
            pl.BlockSpec((PROJ_TM // ATT_TK, 2 * fw, ATT_TK), lambda i: (i, 0, 0)),
            slab(w_o), slab(w_ff1), slab(w_ff2),
        ],
        out_shape=[
            jax.ShapeDtypeStruct((seq, 2 * GROUP_W), BF16),
            jax.ShapeDtypeStruct((seq, 2 * GROUP_W), BF16),
            jax.ShapeDtypeStruct((n_kv, 2 * fw, ATT_TK), BF16),
            jax.ShapeDtypeStruct(w_o.shape, BF16),
            jax.ShapeDtypeStruct(w_ff1.shape, BF16),
            jax.ShapeDtypeStruct(w_ff2.shape, BF16),
        ],
        scratch_shapes=[pltpu.VMEM((8, LANES), F32)],
        compiler_params=pltpu.CompilerParams(
            dimension_semantics=("arbitrary",), vmem_limit_bytes=48 * 1024 * 1024),
        name="proj",
    )(x2, g_mix.reshape(1, d), win, wvt, bf, g_q.reshape(1, Q_LORA), wqup, g_kv.reshape(1, KV_LORA),
      wkup, wvupt, cos_t, sin_t, tri, place, qbias, w_o, w_ff1, w_ff2)

    n_pairs = N_HEADS
    o_all = pl.pallas_call(
        _attn_kernel,
        grid=(n_pairs, seq // ATT_TQ),
        in_specs=[
            pl.BlockSpec((ATT_TQ, 2 * LANES), lambda p, i: (i, p)),
            pl.BlockSpec((seq, 2 * LANES), lambda p, i: (0, p)),
            pl.BlockSpec((n_kv, LANES, ATT_TK), lambda p, i: (0, p, 0)),
            pl.BlockSpec((1, ATT_TK, ATT_TQ), lambda p, i: (p // (N_HEADS // 2), 0, 0)),
        ],
        out_specs=pl.BlockSpec((ATT_TQ, LANES), lambda p, i: (i, p)),
        out_shape=jax.ShapeDtypeStruct((seq, 2 * fw), BF16),
        scratch_shapes=[
            pltpu.VMEM((2, 2, ATT_TK, ATT_TQ), BF16),
            pltpu.VMEM((2, 2, 1, ATT_TQ), F32),
            pltpu.VMEM((2, 2, 1, ATT_TQ), F32),
            pltpu.VMEM((2, LANES, ATT_TQ), F32),
            pltpu.VMEM((2, 1, ATT_TQ), F32),
            pltpu.VMEM((2, 1, ATT_TQ), F32),
            pltpu.VMEM((2, 1, ATT_TQ), F32),
            pltpu.SMEM((1,), F32),
        ],
        compiler_params=pltpu.CompilerParams(
            dimension_semantics=("arbitrary", "arbitrary"), vmem_limit_bytes=48 * 1024 * 1024),
        name="attn",
    )(q_all, k_all, vt_all, _diagonal_masks())

    return pl.pallas_call(
        _out_kernel,
        grid=(seq // OUT_TM,),
        in_specs=[
            pl.BlockSpec((OUT_TM, 2 * fw), row),
            pl.BlockSpec((OUT_TM, d), row),
            _resident((2 * fw, d)), _resident((1, d)), _resident((d, d_ff)), _resident((d_ff, d)),
            _resident((1, d)),
        ],
        out_specs=pl.BlockSpec((OUT_TM, d), row),
        out_shape=jax.ShapeDtypeStruct((seq, d), F32),
        compiler_params=pltpu.CompilerParams(
            dimension_semantics=("arbitrary",), vmem_limit_bytes=56 * 1024 * 1024),
        name="out_mlp",
    )(o_all, x2, wo_b, g_mlp.reshape(1, d), w1_b, w2_b, g_out.reshape(1, d))


def kernel(x, g_mix, w_in, b_f, g_q, w_q_up, g_kv, w_kv_up, w_o, g_mlp, w_ff1, w_ff2, g_final):
    b, seq, d = x.shape
    depth = w_in.shape[0]
    assert b == 1 and depth == 1, "single-sequence, single-layer problem"
    n_steps = seq // PROJ_TM
    assert all(w.shape[1] % (16 * n_steps) == 0 for w in (w_o, w_ff1, w_ff2)), "weight slabs must be bf16-tile aligned"
    assert seq % PROJ_TM == 0 and seq % ATT_TQ == 0 and seq % OUT_TM == 0 and ATT_TQ % ATT_TK == 0
    out = _layer(x[0], g_mix[0], w_in[0], b_f[0], g_q[0], w_q_up[0], g_kv[0], w_kv_up[0], w_o[0],
                 g_mlp[0], w_ff1[0], w_ff2[0], g_final)
    return out[None]
```

```python
import math

import numpy as np
import jax
import jax.numpy as jnp
from jax import lax
from jax.experimental import pallas as pl
from jax.experimental.pallas import tpu as pltpu

F32 = jnp.float32
BF16 = jnp.bfloat16

EPS = 1e-6
CHUNK = 64
N_HEADS = 8
HEAD_DIM = 64
ROPE_DIM = 32
Q_LORA = 256
KV_LORA = 128
ROPE_THETA = 10000.0
LOG2E = math.log2(math.e)
FOX_SCALE = LOG2E / math.sqrt(HEAD_DIM)
MLA_SCALE = LOG2E / math.sqrt(HEAD_DIM + ROPE_DIM)

LANES = 128
MXU_COLS = 256
BIAS_LANE = HEAD_DIM
MAX_JUMP = 60.0

PROJ_TM = 512
ATT_TQ = 512
ATT_TK = 512
OUT_TM = 1024
FF_CHUNK = 1024

GROUP_W = N_HEADS * LANES

FOX_W = N_HEADS * HEAD_DIM
_C_FQ = 0
_C_FK = _C_FQ + FOX_W
_C_GATE = _C_FK + FOX_W
_C_KVLAT = _C_GATE + LANES
_C_QLAT = _C_KVLAT + KV_LORA
TERM_LANES = N_HEADS


def _rms(x, g):
    return x * lax.rsqrt(jnp.mean(x * x, axis=-1, keepdims=True) + EPS) * g


def _split3_packed(a, lane):
    hi = a.astype(BF16).astype(F32)
    r = a - hi
    mid = r.astype(BF16).astype(F32)
    lo = r - mid
    packed = jnp.where(lane < TERM_LANES, hi,
                       jnp.where(lane < 2 * TERM_LANES, pltpu.roll(mid, TERM_LANES, 1),
                                 jnp.where(lane < 3 * TERM_LANES, pltpu.roll(lo, 2 * TERM_LANES, 1), 0.0)))
    return packed.astype(BF16)


def _sum3_packed(a):
    return a + pltpu.roll(a, LANES - TERM_LANES, 1) + pltpu.roll(a, LANES - 2 * TERM_LANES, 1)


def _rope(x, cos, sin, first_half):
    rot = jnp.where(first_half, pltpu.roll(x, LANES - ROPE_DIM // 2, 1), pltpu.roll(x, ROPE_DIM // 2, 1))
    return x * cos + rot * sin


def _spread_pair(pair, low, fill):
    return jnp.where(low, pair, fill), jnp.where(low, pltpu.roll(pair, HEAD_DIM, 1), fill)


def _dot(a, b):
    return jnp.dot(a, b, preferred_element_type=F32)


def _dot_nt(a, b):
    return lax.dot_general(a, b, (((1,), (1,)), ((), ())), preferred_element_type=F32)


def _proj_kernel(x_ref, gmix_ref, win_ref, wvt_ref, bf_ref, gq_ref, wqup_ref, gkv_ref, wkup_ref,
                 wvupt_ref, cos_ref, sin_ref, tri_ref, place_ref, qbias_ref, wo_ref, w1_ref, w2_ref,
                 q_out, k_out, vt_out, wo_out, w1_out, w2_out, carry_ref):
    tm = x_ref.shape[0]

    wo_out[...] = wo_ref[...].astype(BF16)
    w1_out[...] = w1_ref[...].astype(BF16)
    w2_out[...] = w2_ref[...].astype(BF16)

    @pl.when(pl.program_id(0) == 0)
    def _():
        carry_ref[...] = jnp.zeros_like(carry_ref)

    hb = _rms(x_ref[...], gmix_ref[...]).astype(BF16)

    lane = lax.broadcasted_iota(jnp.int32, (tm, LANES), 1)
    low = lane < HEAD_DIM

    gk = _dot(hb, win_ref[:, _C_GATE:_C_GATE + LANES + KV_LORA])
    gate = gk[:, 0:LANES]
    z = gate + bf_ref[...]
    log_f = jnp.minimum(z, 0.0) - jnp.log1p(jnp.exp(-jnp.abs(z)))
    c = _sum3_packed(_dot(tri_ref[...], _split3_packed(log_f, lane))) + carry_ref[0:1, :]
    carry_ref[...] = jnp.broadcast_to(c[tm - 1:tm, :], carry_ref.shape)
    neg_c = _split3_packed(c * -LOG2E, lane)
    for cc in range(FOX_W // MXU_COLS):
        pk = _dot(hb, win_ref[:, _C_FK + cc * MXU_COLS:_C_FK + (cc + 1) * MXU_COLS])
        for jj in range(MXU_COLS // LANES):
            j = cc * (MXU_COLS // LANES) + jj
            even, odd = 2 * j * LANES, (2 * j + 1) * LANES
            placed = _dot(neg_c, place_ref[:, even:even + 2 * LANES])
            pair = pk[:, jj * LANES:(jj + 1) * LANES]
            k_out[:, even:even + LANES] = jnp.where(low, pair, placed[:, 0:LANES]).astype(BF16)
            k_out[:, odd:odd + LANES] = jnp.where(
                low, pltpu.roll(pair, HEAD_DIM, 1), placed[:, LANES:2 * LANES]).astype(BF16)

    for cc in range(FOX_W // MXU_COLS):
        pq = _dot(hb, win_ref[:, _C_FQ + cc * MXU_COLS:_C_FQ + (cc + 1) * MXU_COLS]) * FOX_SCALE
        for jj in range(MXU_COLS // LANES):
            j = cc * (MXU_COLS // LANES) + jj
            even, odd = _spread_pair(pq[:, jj * LANES:(jj + 1) * LANES], low, qbias_ref[...])
            q_out[:, 2 * j * LANES:(2 * j + 1) * LANES] = even.astype(BF16)
            q_out[:, (2 * j + 1) * LANES:(2 * j + 2) * LANES] = odd.astype(BF16)

    cos = cos_ref[...]
    sin = sin_ref[...]
    first_half = lane < HEAD_DIM + ROPE_DIM // 2

    q_lat = _dot(hb, win_ref[:, _C_QLAT:_C_QLAT + Q_LORA])
    nq = _rms(q_lat, gq_ref[...]).astype(BF16)
    cos_q, sin_q = cos * MLA_SCALE, sin * MLA_SCALE
    for cc in range(GROUP_W // MXU_COLS):
        cq = _dot(nq, wqup_ref[:, cc * MXU_COLS:(cc + 1) * MXU_COLS])
        for hh in range(MXU_COLS // LANES):
            h = cc * (MXU_COLS // LANES) + hh
            xg = cq[:, hh * LANES:(hh + 1) * LANES]
            q_out[:, GROUP_W + h * LANES:GROUP_W + (h + 1) * LANES] = _rope(
                xg, cos_q, sin_q, first_half).astype(BF16)

    nkv = _rms(gk[:, LANES:LANES + KV_LORA], gkv_ref[...]).astype(BF16)
    kr = _rope(jnp.where(low, 0.0, gate), cos, sin, first_half)
    for cc in range(GROUP_W // MXU_COLS):
        kn = _dot(nkv, wkup_ref[:, cc * MXU_COLS:(cc + 1) * MXU_COLS])
        for hh in range(MXU_COLS // LANES):
            h = cc * (MXU_COLS // LANES) + hh
            k_out[:, GROUP_W + h * LANES:GROUP_W + (h + 1) * LANES] = (
                kn[:, hh * LANES:(hh + 1) * LANES] + kr).astype(BF16)

    vf = _dot_nt(wvt_ref[...], hb).astype(BF16)
    vm = _dot_nt(wvupt_ref[...], nkv).astype(BF16)
    nv = vf.shape[0]
    for b in range(tm // ATT_TK):
        vt_out[b, 0:nv, :] = vf[:, b * ATT_TK:(b + 1) * ATT_TK]
        vt_out[b, nv:2 * nv, :] = vm[:, b * ATT_TK:(b + 1) * ATT_TK]


def _attn_kernel(q_ref, k_ref, vt_ref, dmask_ref, o_ref, p_ref, cs_ref, beta_ref, acc_ref, m_ref, l_ref, jump_ref,
                 flag_ref):
    tq = q_ref.shape[0]
    tk = vt_ref.shape[2]
    assert tq == tk
    i = pl.program_id(1)

    def qk(t, h):
        kh = k_ref[pl.ds(pl.multiple_of(t * tk, tk), tk), h * LANES:(h + 1) * LANES]
        return _dot_nt(kh, q_ref[:, h * LANES:(h + 1) * LANES])

    def diagonal_scores(h):
        return qk(i, h) + dmask_ref[0]

    l_ref[...] = jnp.zeros_like(l_ref)
    acc_ref[...] = jnp.zeros_like(acc_ref)

    def stage_a_diagonal():
        hk, hq = tk // 2, tq // 2
        assert hk % CHUNK == 0 and hk == hq
        for h in range(2):
            qh = q_ref[:, h * LANES:(h + 1) * LANES]
            k0 = pl.multiple_of(i * tk, tk)
            st_top = _dot_nt(k_ref[pl.ds(k0, hk), h * LANES:(h + 1) * LANES], qh) + dmask_ref[0, 0:hk, :]
            st_bot = (_dot_nt(k_ref[pl.ds(k0 + hk, hk), h * LANES:(h + 1) * LANES], qh[hq:, :])
                      + dmask_ref[0, hk:, hq:])
            m_early = jnp.max(st_top[:, :hq], axis=0, keepdims=True)
            m_late = jnp.maximum(jnp.max(st_top[:, hq:], axis=0, keepdims=True),
                                 jnp.max(st_bot, axis=0, keepdims=True))
            m0 = jnp.concatenate([m_early, m_late], axis=1)
            p_top = jnp.exp2(st_top - m0)
            p_bot = jnp.exp2(st_bot - m_late)
            p_ref[0, h, 0:hk, :] = p_top.astype(BF16)
            p_ref[0, h, hk:, 0:hq] = jnp.zeros((hk, hq), BF16)
            p_ref[0, h, hk:, hq:] = p_bot.astype(BF16)
            cs_ref[0, h] = jnp.concatenate(
                [jnp.sum(p_top[:, :hq], axis=0, keepdims=True),
                 jnp.sum(p_top[:, hq:], axis=0, keepdims=True) + jnp.sum(p_bot, axis=0, keepdims=True)], axis=1)
            beta_ref[0, h] = jnp.ones_like(m0)
            jump_ref[h] = jnp.zeros_like(m0)
            m_ref[h] = m0

    def stage_a(u, slot, heads=(0, 1)):
        for h in heads:
            st = qk(i - u, h)
            m_old = m_ref[h]
            pt = jnp.exp2(st - m_old)
            p_ref[slot, h] = pt.astype(BF16)
            cs_ref[slot, h] = jnp.sum(pt, axis=0, keepdims=True)
            tile_max = jnp.max(st, axis=0, keepdims=True)
            m_new = jnp.maximum(m_old, tile_max)
            beta_ref[slot, h] = jnp.exp2(m_old - m_new)
            jump_ref[h] = jnp.maximum(jump_ref[h], tile_max - m_old)
            m_ref[h] = m_new

    def stage_u(u, slot, heads=(0, 1)):
        for h in heads:
            beta = beta_ref[slot, h]
            acc_ref[h] = (acc_ref[h] + _dot(vt_ref[i - u], p_ref[slot, h])) * beta
            l_ref[h] = (l_ref[h] + cs_ref[slot, h]) * beta

    def full_tiles(u0, n):
        for k in range(n):
            for h in range(2):
                stage_a(u0 + k + 1, (k + 1) % 2, (h,))
                stage_u(u0 + k, k % 2, (h,))

    stage_a_diagonal()

    def eight_tiles(jj, carry):
        full_tiles(8 * jj, 8)
        return carry

    lax.fori_loop(0, i // 8, eight_tiles, 0)

    @pl.when(i % 8 >= 4)
    def _():
        full_tiles(8 * (i // 8), 4)

    @pl.when(i % 4 >= 2)
    def _():
        full_tiles(4 * (i // 4), 2)

    def finish():
        o0 = acc_ref[0] * (1.0 / l_ref[0])
        o1 = acc_ref[1] * (1.0 / l_ref[1])
        row = lax.broadcasted_iota(jnp.int32, o0.shape, 0)
        ot = jnp.where(row < HEAD_DIM, o0, o1)
        o_ref[...] = ot.T.astype(BF16)

    @pl.when(i % 2 == 1)
    def _():
        full_tiles(i - 1, 1)
        flag_ref[0] = jnp.max(jump_ref[...])
        stage_u(i, 1)
        finish()

    @pl.when(i % 2 == 0)
    def _():
        flag_ref[0] = jnp.max(jump_ref[...])
        stage_u(i, 0)
        finish()

    @pl.when(flag_ref[0] > MAX_JUMP)
    def _():
        m_ref[...] = jnp.full_like(m_ref, -jnp.inf)
        l_ref[...] = jnp.zeros_like(l_ref)
        acc_ref[...] = jnp.zeros_like(acc_ref)

        def two_pass(t, st, h):
            m_prev = m_ref[h]
            m_new = jnp.maximum(m_prev, jnp.max(st, axis=0, keepdims=True))
            alpha = jnp.exp2(m_prev - m_new)
            pt = jnp.exp2(st - m_new)
            l_ref[h] = alpha * l_ref[h] + jnp.sum(pt, axis=0, keepdims=True)
            acc_ref[h] = alpha * acc_ref[h] + _dot(vt_ref[t], pt.astype(BF16))
            m_ref[h] = m_new

        def full_tile(t, carry):
            for h in range(2):
                two_pass(t, qk(t, h), h)
            return carry

        lax.fori_loop(0, i, full_tile, 0)
        for h in range(2):
            two_pass(i, diagonal_scores(h), h)
        finish()


def _out_kernel(o_ref, x_ref, wo_ref, gmlp_ref, w1_ref, w2_ref, gfin_ref, out_ref):
    x1 = x_ref[...] + _dot(o_ref[...], wo_ref[...])
    h2 = _rms(x1, gmlp_ref[...]).astype(BF16)
    y = x1
    for c in range(w1_ref.shape[1] // FF_CHUNK):
        u = _dot(h2, w1_ref[:, c * FF_CHUNK:(c + 1) * FF_CHUNK])
        a = jnp.square(jnp.maximum(u, 0.0)).astype(BF16)
        y = y + _dot(a, w2_ref[c * FF_CHUNK:(c + 1) * FF_CHUNK, :])
    out_ref[...] = _rms(y, gfin_ref[...])


def _pad_heads(w, head_w):
    k = w.shape[0]
    w3 = w.reshape(k, N_HEADS, head_w)
    return jnp.pad(w3, ((0, 0), (0, 0), (0, LANES - head_w))).reshape(k, GROUP_W)


def _rope_tables(seq):
    half = ROPE_DIM // 2
    inv = ROPE_THETA ** (-np.arange(half, dtype=np.float64) / half)
    ang = np.arange(seq, dtype=np.float64)[:, None] * inv[None, :]
    cos, sin = np.cos(ang), np.sin(ang)
    cos_t = np.ones((seq, LANES), np.float64)
    sin_t = np.zeros((seq, LANES), np.float64)
    cos_t[:, HEAD_DIM:HEAD_DIM + half] = cos
    cos_t[:, HEAD_DIM + half:HEAD_DIM + ROPE_DIM] = cos
    sin_t[:, HEAD_DIM:HEAD_DIM + half] = -sin
    sin_t[:, HEAD_DIM + half:HEAD_DIM + ROPE_DIM] = sin
    return jnp.asarray(cos_t, F32), jnp.asarray(sin_t, F32)


def _placement():
    e = np.zeros((LANES, GROUP_W), np.float32)
    for t in range(3):
        for h in range(N_HEADS):
            e[t * TERM_LANES + h, h * LANES + BIAS_LANE + t] = 1.0
    qb = np.zeros((1, LANES), np.float32)
    qb[0, BIAS_LANE:BIAS_LANE + 3] = 1.0
    return jnp.asarray(e, BF16), jnp.asarray(qb, F32)


def _diagonal_masks():
    kpos = np.arange(ATT_TK)[:, None]
    qpos = np.arange(ATT_TQ)[None, :]
    visible = np.stack([kpos <= qpos, kpos // CHUNK <= qpos // CHUNK])
    return jnp.asarray(np.where(visible, 0.0, -np.inf), F32)


def _resident(shape):
    nd = len(shape)
    return pl.BlockSpec(shape, lambda *_: (0,) * nd, pipeline_mode=pl.Buffered(1))


def _layer(x2, g_mix, w_in, b_f, g_q, w_q_up, g_kv, w_kv_up, w_o, g_mlp, w_ff1, w_ff2, g_out):
    seq, d = x2.shape
    fw = FOX_W
    splits = np.cumsum([fw, fw, fw, N_HEADS, Q_LORA, KV_LORA])
    w_fq, w_fk, w_fv, w_lg, w_ql, w_kvl, w_kr = jnp.split(w_in, splits, axis=1)
    w_kr_g = jnp.pad(w_kr, ((0, 0), (HEAD_DIM, LANES - HEAD_DIM - ROPE_DIM)))
    w_gate = w_kr_g.at[:, 0:N_HEADS].set(w_lg)
    win = jnp.concatenate([w_fq, w_fk, w_gate, w_kvl, w_ql], axis=1).astype(BF16)
    wvt = w_fv.T.astype(BF16)
    bf = jnp.pad(b_f, (0, LANES - N_HEADS)).reshape(1, LANES)
    wqup = _pad_heads(w_q_up, HEAD_DIM + ROPE_DIM).astype(BF16)
    wkv3 = w_kv_up.reshape(KV_LORA, N_HEADS, 2 * HEAD_DIM)
    wkup = _pad_heads(wkv3[:, :, :HEAD_DIM].reshape(KV_LORA, fw), HEAD_DIM).astype(BF16)
    wvupt = wkv3[:, :, HEAD_DIM:].reshape(KV_LORA, fw).T.astype(BF16)
    cos_t, sin_t = _rope_tables(seq)
    tri = jnp.asarray(np.tril(np.ones((PROJ_TM, PROJ_TM), np.float32)), BF16)
    place, qbias = _placement()

    n_kv = seq // ATT_TK
    row = lambda i: (i, 0)
    n_steps = seq // PROJ_TM
    d_ff = w_ff1.shape[1]
    slab = lambda w: pl.BlockSpec((w.shape[0] // n_steps, w.shape[1]), row)
    q_all, k_all, vt_all, wo_b, w1_b, w2_b = pl.pallas_call(
        _proj_kernel,
        grid=(seq // PROJ_TM,),
        in_specs=[
            pl.BlockSpec((PROJ_TM, d), row),
            _resident((1, d)), _resident(win.shape), _resident(wvt.shape), _resident((1, LANES)),
            _resident((1, Q_LORA)), _resident(wqup.shape), _resident((1, KV_LORA)), _resident(wkup.shape),
            _resident(wvupt.shape),
            pl.BlockSpec((PROJ_TM, LANES), row), pl.BlockSpec((PROJ_TM, LANES), row),
            _resident(tri.shape), _resident(place.shape), _resident(qbias.shape),
            slab(w_o), slab(w_ff1), slab(w_ff2),
        ],
        out_specs=[
            pl.BlockSpec((PROJ_TM, 2 * GROUP_W), row),
            pl.BlockSpec((PROJ_TM, 2 * GROUP_W), row),
            pl.BlockSpec((PROJ_TM // ATT_TK, 2 * fw, ATT_TK), lambda i: (i, 0, 0)),
            slab(w_o), slab(w_ff1), slab(w_ff2),
        ],
        out_shape=[
            jax.ShapeDtypeStruct((seq, 2 * GROUP_W), BF16),
            jax.ShapeDtypeStruct((seq, 2 * GROUP_W), BF16),
            jax.ShapeDtypeStruct((n_kv, 2 * fw, ATT_TK), BF16),
            jax.ShapeDtypeStruct(w_o.shape, BF16),
            jax.ShapeDtypeStruct(w_ff1.shape, BF16),
            jax.ShapeDtypeStruct(w_ff2.shape, BF16),
        ],
        scratch_shapes=[pltpu.VMEM((8, LANES), F32)],
        compiler_params=pltpu.CompilerParams(
            dimension_semantics=("arbitrary",), vmem_limit_bytes=48 * 1024 * 1024),
        name="proj",
    )(x2, g_mix.reshape(1, d), win, wvt, bf, g_q.reshape(1, Q_LORA), wqup, g_kv.reshape(1, KV_LORA),
      wkup, wvupt, cos_t, sin_t, tri, place, qbias, w_o, w_ff1, w_ff2)

    n_pairs = N_HEADS
    o_all = pl.pallas_call(
        _attn_kernel,
        grid=(n_pairs, seq // ATT_TQ),
        in_specs=[
            pl.BlockSpec((ATT_TQ, 2 * LANES), lambda p, i: (i, p)),
            pl.BlockSpec((seq, 2 * LANES), lambda p, i: (0, p)),
            pl.BlockSpec((n_kv, LANES, ATT_TK), lambda p, i: (0, p, 0)),
            pl.BlockSpec((1, ATT_TK, ATT_TQ), lambda p, i: (p // (N_HEADS // 2), 0, 0)),
        ],
        out_specs=pl.BlockSpec((ATT_TQ, LANES), lambda p, i: (i, p)),
        out_shape=jax.ShapeDtypeStruct((seq, 2 * fw), BF16),
        scratch_shapes=[
            pltpu.VMEM((2, 2, ATT_TK, ATT_TQ), BF16),
            pltpu.VMEM((2, 2, 1, ATT_TQ), F32),
            pltpu.VMEM((2, 2, 1, ATT_TQ), F32),
            pltpu.VMEM((2, LANES, ATT_TQ), F32),
            pltpu.VMEM((2, 1, ATT_TQ), F32),
            pltpu.VMEM((2, 1, ATT_TQ), F32),
            pltpu.VMEM((2, 1, ATT_TQ), F32),
            pltpu.SMEM((1,), F32),
        ],
        compiler_params=pltpu.CompilerParams(
            dimension_semantics=("arbitrary", "arbitrary"), vmem_limit_bytes=48 * 1024 * 1024),
        name="attn",
    )(q_all, k_all, vt_all, _diagonal_masks())

    return pl.pallas_call(
        _out_kernel,
        grid=(seq // OUT_TM,),
        in_specs=[
            pl.BlockSpec((OUT_TM, 2 * fw), row),
            pl.BlockSpec((OUT_TM, d), row),
            _resident((2 * fw, d)), _resident((1, d)), _resident((d, d_ff)), _resident((d_ff, d)),
            _resident((1, d)),
        ],
        out_specs=pl.BlockSpec((OUT_TM, d), row),
        out_shape=jax.ShapeDtypeStruct((seq, d), F32),
        compiler_params=pltpu.CompilerParams(
            dimension_semantics=("arbitrary",), vmem_limit_bytes=56 * 1024 * 1024),
        name="out_mlp",
    )(o_all, x2, wo_b, g_mlp.reshape(1, d), w1_b, w2_b, g_out.reshape(1, d))


def kernel(x, g_mix, w_in, b_f, g_q, w_q_up, g_kv, w_kv_up, w_o, g_mlp, w_ff1, w_ff2, g_final):
    b, seq, d = x.shape
    depth = w_in.shape[0]
    assert b == 1 and depth == 1, "single-sequence, single-layer problem"
    n_steps = seq // PROJ_TM
    assert all(w.shape[1] % (16 * n_steps) == 0 for w in (w_o, w_ff1, w_ff2)), "weight slabs must be bf16-tile aligned"
    assert seq % PROJ_TM == 0 and seq % ATT_TQ == 0 and seq % OUT_TM == 0 and ATT_TQ % ATT_TK == 0
    out = _layer(x[0], g_mix[0], w_in[0], b_f[0], g_q[0], w_q_up[0], g_kv[0], w_kv_up[0], w_o[0],
                 g_mlp[0], w_ff1[0], w_ff2[0], g_final)
    return out[None]
```

```python
import math

import numpy as np
import jax
import jax.numpy as jnp
from jax import lax
from jax.experimental import pallas as pl
from jax.experimental.pallas import tpu as pltpu

F32 = jnp.float32
BF16 = jnp.bfloat16

EPS = 1e-6
CHUNK = 64
N_HEADS = 8
HEAD_DIM = 64
ROPE_DIM = 32
Q_LORA = 256
KV_LORA = 128
ROPE_THETA = 10000.0
LOG2E = math.log2(math.e)
FOX_SCALE = LOG2E / math.sqrt(HEAD_DIM)
MLA_SCALE = LOG2E / math.sqrt(HEAD_DIM + ROPE_DIM)

LANES = 128
MXU_COLS = 256
BIAS_LANE = HEAD_DIM
MAX_JUMP = 60.0

PROJ_TM = 512
ATT_TQ = 512
ATT_TK = 512
OUT_TM = 1024
FF_CHUNK = 1024

GROUP_W = N_HEADS * LANES

FOX_W = N_HEADS * HEAD_DIM
_C_FQ = 0
_C_FK = _C_FQ + FOX_W
_C_GATE = _C_FK + FOX_W
_C_KVLAT = _C_GATE + LANES
_C_QLAT = _C_KVLAT + KV_LORA
TERM_LANES = N_HEADS


def _rms(x, g):
    return x * lax.rsqrt(jnp.mean(x * x, axis=-1, keepdims=True) + EPS) * g


def _split3_packed(a, lane):
    hi = a.astype(BF16).astype(F32)
    r = a - hi
    mid = r.astype(BF16).astype(F32)
    lo = r - mid
    packed = jnp.where(lane < TERM_LANES, hi,
                       jnp.where(lane < 2 * TERM_LANES, pltpu.roll(mid, TERM_LANES, 1),
                                 jnp.where(lane < 3 * TERM_LANES, pltpu.roll(lo, 2 * TERM_LANES, 1), 0.0)))
    return packed.astype(BF16)


def _sum3_packed(a):
    return a + pltpu.roll(a, LANES - TERM_LANES, 1) + pltpu.roll(a, LANES - 2 * TERM_LANES, 1)


def _rope(x, cos, sin, first_half):
    rot = jnp.where(first_half, pltpu.roll(x, LANES - ROPE_DIM // 2, 1), pltpu.roll(x, ROPE_DIM // 2, 1))
    return x * cos + rot * sin


def _spread_pair(pair, low, fill):
    return jnp.where(low, pair, fill), jnp.where(low, pltpu.roll(pair, HEAD_DIM, 1), fill)


def _dot(a, b):
    return jnp.dot(a, b, preferred_element_type=F32)


def _dot_nt(a, b):
    return lax.dot_general(a, b, (((1,), (1,)), ((), ())), preferred_element_type=F32)


def _proj_kernel(x_ref, gmix_ref, win_ref, wvt_ref, bf_ref, gq_ref, wqup_ref, gkv_ref, wkup_ref,
                 wvupt_ref, cos_ref, sin_ref, tri_ref, place_ref, qbias_ref, wo_ref, w1_ref, w2_ref,
                 q_out, k_out, vt_out, wo_out, w1_out, w2_out, carry_ref):
    tm = x_ref.shape[0]

    wo_out[...] = wo_ref[...].astype(BF16)
    w1_out[...] = w1_ref[...].astype(BF16)
    w2_out[...] = w2_ref[...].astype(BF16)

    @pl.when(pl.program_id(0) == 0)
    def _():
        carry_ref[...] = jnp.zeros_like(carry_ref)

    hb = _rms(x_ref[...], gmix_ref[...]).astype(BF16)

    lane = lax.broadcasted_iota(jnp.int32, (tm, LANES), 1)
    low = lane < HEAD_DIM

    gk = _dot(hb, win_ref[:, _C_GATE:_C_GATE + LANES + KV_LORA])
    gate = gk[:, 0:LANES]
    z = gate + bf_ref[...]
    log_f = jnp.minimum(z, 0.0) - jnp.log1p(jnp.exp(-jnp.abs(z)))
    c = _sum3_packed(_dot(tri_ref[...], _split3_packed(log_f, lane))) + carry_ref[0:1, :]
    carry_ref[...] = jnp.broadcast_to(c[tm - 1:tm, :], carry_ref.shape)
    neg_c = _split3_packed(c * -LOG2E, lane)
    for cc in range(FOX_W // MXU_COLS):
        pk = _dot(hb, win_ref[:, _C_FK + cc * MXU_COLS:_C_FK + (cc + 1) * MXU_COLS])
        for jj in range(MXU_COLS // LANES):
            j = cc * (MXU_COLS // LANES) + jj
            even, odd = 2 * j * LANES, (2 * j + 1) * LANES
            placed = _dot(neg_c, place_ref[:, even:even + 2 * LANES])
            pair = pk[:, jj * LANES:(jj + 1) * LANES]
            k_out[:, even:even + LANES] = jnp.where(low, pair, placed[:, 0:LANES]).astype(BF16)
            k_out[:, odd:odd + LANES] = jnp.where(
                low, pltpu.roll(pair, HEAD_DIM, 1), placed[:, LANES:2 * LANES]).astype(BF16)

    for cc in range(FOX_W // MXU_COLS):
        pq = _dot(hb, win_ref[:, _C_FQ + cc * MXU_COLS:_C_FQ + (cc + 1) * MXU_COLS]) * FOX_SCALE
        for jj in range(MXU_COLS // LANES):
            j = cc * (MXU_COLS // LANES) + jj
            even, odd = _spread_pair(pq[:, jj * LANES:(jj + 1) * LANES], low, qbias_ref[...])
            q_out[:, 2 * j * LANES:(2 * j + 1) * LANES] = even.astype(BF16)
            q_out[:, (2 * j + 1) * LANES:(2 * j + 2) * LANES] = odd.astype(BF16)

    cos = cos_ref[...]
    sin = sin_ref[...]
    first_half = lane < HEAD_DIM + ROPE_DIM // 2

    q_lat = _dot(hb, win_ref[:, _C_QLAT:_C_QLAT + Q_LORA])
    nq = _rms(q_lat, gq_ref[...]).astype(BF16)
    cos_q, sin_q = cos * MLA_SCALE, sin * MLA_SCALE
    for cc in range(GROUP_W // MXU_COLS):
        cq = _dot(nq, wqup_ref[:, cc * MXU_COLS:(cc + 1) * MXU_COLS])
        for hh in range(MXU_COLS // LANES):
            h = cc * (MXU_COLS // LANES) + hh
            xg = cq[:, hh * LANES:(hh + 1) * LANES]
            q_out[:, GROUP_W + h * LANES:GROUP_W + (h + 1) * LANES] = _rope(
                xg, cos_q, sin_q, first_half).astype(BF16)

    nkv = _rms(gk[:, LANES:LANES + KV_LORA], gkv_ref[...]).astype(BF16)
    kr = _rope(jnp.where(low, 0.0, gate), cos, sin, first_half)
    for cc in range(GROUP_W // MXU_COLS):
        kn = _dot(nkv, wkup_ref[:, cc * MXU_COLS:(cc + 1) * MXU_COLS])
        for hh in range(MXU_COLS // LANES):
            h = cc * (MXU_COLS // LANES) + hh
            k_out[:, GROUP_W + h * LANES:GROUP_W + (h + 1) * LANES] = (
                kn[:, hh * LANES:(hh + 1) * LANES] + kr).astype(BF16)

    vf = _dot_nt(wvt_ref[...], hb).astype(BF16)
    vm = _dot_nt(wvupt_ref[...], nkv).astype(BF16)
    nv = vf.shape[0]
    for b in range(tm // ATT_TK):
        vt_out[b, 0:nv, :] = vf[:, b * ATT_TK:(b + 1) * ATT_TK]
        vt_out[b, nv:2 * nv, :] = vm[:, b * ATT_TK:(b + 1) * ATT_TK]


def _attn_kernel(q_ref, k_ref, vt_ref, dmask_ref, o_ref, p_ref, cs_ref, beta_ref, acc_ref, m_ref, l_ref, jump_ref,
                 flag_ref):
    tq = q_ref.shape[0]
    tk = vt_ref.shape[2]
    assert tq == tk
    i = pl.program_id(1)

    def qk(t, h):
        kh = k_ref[pl.ds(pl.multiple_of(t * tk, tk), tk), h * LANES:(h + 1) * LANES]
        return _dot_nt(kh, q_ref[:, h * LANES:(h + 1) * LANES])

    def diagonal_scores(h):
        return qk(i, h) + dmask_ref[0]

    l_ref[...] = jnp.zeros_like(l_ref)
    acc_ref[...] = jnp.zeros_like(acc_ref)

    def stage_a_diagonal():
        hk, hq = tk // 2, tq // 2
        assert hk % CHUNK == 0 and hk == hq
        for h in range(2):
            qh = q_ref[:, h * LANES:(h + 1) * LANES]
            k0 = pl.multiple_of(i * tk, tk)
            st_top = _dot_nt(k_ref[pl.ds(k0, hk), h * LANES:(h + 1) * LANES], qh) + dmask_ref[0, 0:hk, :]
            st_bot = (_dot_nt(k_ref[pl.ds(k0 + hk, hk), h * LANES:(h + 1) * LANES], qh[hq:, :])
                      + dmask_ref[0, hk:, hq:])
            m_early = jnp.max(st_top[:, :hq], axis=0, keepdims=True)
            m_late = jnp.maximum(jnp.max(st_top[:, hq:], axis=0, keepdims=True),
                                 jnp.max(st_bot, axis=0, keepdims=True))
            m0 = jnp.concatenate([m_early, m_late], axis=1)
            p_top = jnp.exp2(st_top - m0)
            p_bot = jnp.exp2(st_bot - m_late)
            p_ref[0, h, 0:hk, :] = p_top.astype(BF16)
            p_ref[0, h, hk:, 0:hq] = jnp.zeros((hk, hq), BF16)
            p_ref[0, h, hk:, hq:] = p_bot.astype(BF16)
            cs_ref[0, h] = jnp.concatenate(
                [jnp.sum(p_top[:, :hq], axis=0, keepdims=True),
                 jnp.sum(p_top[:, hq:], axis=0, keepdims=True) + jnp.sum(p_bot, axis=0, keepdims=True)], axis=1)
            beta_ref[0, h] = jnp.ones_like(m0)
            jump_ref[h] = jnp.zeros_like(m0)
            m_ref[h] = m0

    def stage_a(u, slot, heads=(0, 1)):
        for h in heads:
            st = qk(i - u, h)
            m_old = m_ref[h]
            pt = jnp.exp2(st - m_old)
            p_ref[slot, h] = pt.astype(BF16)
            cs_ref[slot, h] = jnp.sum(pt, axis=0, keepdims=True)
            tile_max = jnp.max(st, axis=0, keepdims=True)
            m_new = jnp.maximum(m_old, tile_max)
            beta_ref[slot, h] = jnp.exp2(m_old - m_new)
            jump_ref[h] = jnp.maximum(jump_ref[h], tile_max - m_old)
            m_ref[h] = m_new

    def stage_u(u, slot, heads=(0, 1)):
        for h in heads:
            beta = beta_ref[slot, h]
            acc_ref[h] = (acc_ref[h] + _dot(vt_ref[i - u], p_ref[slot, h])) * beta
            l_ref[h] = (l_ref[h] + cs_ref[slot, h]) * beta

    def full_tiles(u0, n):
        for k in range(n):
            for h in range(2):
                stage_a(u0 + k + 1, (k + 1) % 2, (h,))
                stage_u(u0 + k, k % 2, (h,))

    stage_a_diagonal()

    def eight_tiles(jj, carry):
        full_tiles(8 * jj, 8)
        return carry

    lax.fori_loop(0, i // 8, eight_tiles, 0)

    @pl.when(i % 8 >= 4)
    def _():
        full_tiles(8 * (i // 8), 4)

    @pl.when(i % 4 >= 2)
    def _():
        full_tiles(4 * (i // 4), 2)

    def finish():
        o0 = acc_ref[0] * (1.0 / l_ref[0])
        o1 = acc_ref[1] * (1.0 / l_ref[1])
        row = lax.broadcasted_iota(jnp.int32, o0.shape, 0)
        ot = jnp.where(row < HEAD_DIM, o0, o1)
        o_ref[...] = ot.astype(BF16)

    @pl.when(i % 2 == 1)
    def _():
        full_tiles(i - 1, 1)
        flag_ref[0] = jnp.max(jump_ref[...])
        stage_u(i, 1)
        finish()

    @pl.when(i % 2 == 0)
    def _():
        flag_ref[0] = jnp.max(jump_ref[...])
        stage_u(i, 0)
        finish()

    @pl.when(flag_ref[0] > MAX_JUMP)
    def _():
        m_ref[...] = jnp.full_like(m_ref, -jnp.inf)
        l_ref[...] = jnp.zeros_like(l_ref)
        acc_ref[...] = jnp.zeros_like(acc_ref)

        def two_pass(t, st, h):
            m_prev = m_ref[h]
            m_new = jnp.maximum(m_prev, jnp.max(st, axis=0, keepdims=True))
            alpha = jnp.exp2(m_prev - m_new)
            pt = jnp.exp2(st - m_new)
            l_ref[h] = alpha * l_ref[h] + jnp.sum(pt, axis=0, keepdims=True)
            acc_ref[h] = alpha * acc_ref[h] + _dot(vt_ref[t], pt.astype(BF16))
            m_ref[h] = m_new

        def full_tile(t, carry):
            for h in range(2):
                two_pass(t, qk(t, h), h)
            return carry

        lax.fori_loop(0, i, full_tile, 0)
        for h in range(2):
            two_pass(i, diagonal_scores(h), h)
        finish()


def _out_kernel(o_ref, x_ref, wo_ref, gmlp_ref, w1_ref, w2_ref, gfin_ref, out_ref):
    x1 = x_ref[...] + lax.dot_general(o_ref[...], wo_ref[...], (((0,), (0,)), ((), ())),
                                      preferred_element_type=F32)
    h2 = _rms(x1, gmlp_ref[...]).astype(BF16)
    y = x1
    for c in range(w1_ref.shape[1] // FF_CHUNK):
        u = _dot(h2, w1_ref[:, c * FF_CHUNK:(c + 1) * FF_CHUNK])
        a = jnp.square(jnp.maximum(u, 0.0)).astype(BF16)
        y = y + _dot(a, w2_ref[c * FF_CHUNK:(c + 1) * FF_CHUNK, :])
    out_ref[...] = _rms(y, gfin_ref[...])


def _pad_heads(w, head_w):
    k = w.shape[0]
    w3 = w.reshape(k, N_HEADS, head_w)
    return jnp.pad(w3, ((0, 0), (0, 0), (0, LANES - head_w))).reshape(k, GROUP_W)


def _rope_tables(seq):
    half = ROPE_DIM // 2
    inv = ROPE_THETA ** (-np.arange(half, dtype=np.float64) / half)
    ang = np.arange(seq, dtype=np.float64)[:, None] * inv[None, :]
    cos, sin = np.cos(ang), np.sin(ang)
    cos_t = np.ones((seq, LANES), np.float64)
    sin_t = np.zeros((seq, LANES), np.float64)
    cos_t[:, HEAD_DIM:HEAD_DIM + half] = cos
    cos_t[:, HEAD_DIM + half:HEAD_DIM + ROPE_DIM] = cos
    sin_t[:, HEAD_DIM:HEAD_DIM + half] = -sin
    sin_t[:, HEAD_DIM + half:HEAD_DIM + ROPE_DIM] = sin
    return jnp.asarray(cos_t, F32), jnp.asarray(sin_t, F32)


def _placement():
    e = np.zeros((LANES, GROUP_W), np.float32)
    for t in range(3):
        for h in range(N_HEADS):
            e[t * TERM_LANES + h, h * LANES + BIAS_LANE + t] = 1.0
    qb = np.zeros((1, LANES), np.float32)
    qb[0, BIAS_LANE:BIAS_LANE + 3] = 1.0
    return jnp.asarray(e, BF16), jnp.asarray(qb, F32)


def _diagonal_masks():
    kpos = np.arange(ATT_TK)[:, None]
    qpos = np.arange(ATT_TQ)[None, :]
    visible = np.stack([kpos <= qpos, kpos // CHUNK <= qpos // CHUNK])
    return jnp.asarray(np.where(visible, 0.0, -np.inf), F32)


def _resident(shape):
    nd = len(shape)
    return pl.BlockSpec(shape, lambda *_: (0,) * nd, pipeline_mode=pl.Buffered(1))


def _layer(x2, g_mix, w_in, b_f, g_q, w_q_up, g_kv, w_kv_up, w_o, g_mlp, w_ff1, w_ff2, g_out):
    seq, d = x2.shape
    fw = FOX_W
    splits = np.cumsum([fw, fw, fw, N_HEADS, Q_LORA, KV_LORA])
    w_fq, w_fk, w_fv, w_lg, w_ql, w_kvl, w_kr = jnp.split(w_in, splits, axis=1)
    w_kr_g = jnp.pad(w_kr, ((0, 0), (HEAD_DIM, LANES - HEAD_DIM - ROPE_DIM)))
    w_gate = w_kr_g.at[:, 0:N_HEADS].set(w_lg)
    win = jnp.concatenate([w_fq, w_fk, w_gate, w_kvl, w_ql], axis=1).astype(BF16)
    wvt = w_fv.T.astype(BF16)
    bf = jnp.pad(b_f, (0, LANES - N_HEADS)).reshape(1, LANES)
    wqup = _pad_heads(w_q_up, HEAD_DIM + ROPE_DIM).astype(BF16)
    wkv3 = w_kv_up.reshape(KV_LORA, N_HEADS, 2 * HEAD_DIM)
    wkup = _pad_heads(wkv3[:, :, :HEAD_DIM].reshape(KV_LORA, fw), HEAD_DIM).astype(BF16)
    wvupt = wkv3[:, :, HEAD_DIM:].reshape(KV_LORA, fw).T.astype(BF16)
    cos_t, sin_t = _rope_tables(seq)
    tri = jnp.asarray(np.tril(np.ones((PROJ_TM, PROJ_TM), np.float32)), BF16)
    place, qbias = _placement()

    n_kv = seq // ATT_TK
    row = lambda i: (i, 0)
    n_steps = seq // PROJ_TM
    d_ff = w_ff1.shape[1]
    slab = lambda w: pl.BlockSpec((w.shape[0] // n_steps, w.shape[1]), row)
    q_all, k_all, vt_all, wo_b, w1_b, w2_b = pl.pallas_call(
        _proj_kernel,
        grid=(seq // PROJ_TM,),
        in_specs=[
            pl.BlockSpec((PROJ_TM, d), row),
            _resident((1, d)), _resident(win.shape), _resident(wvt.shape), _resident((1, LANES)),
            _resident((1, Q_LORA)), _resident(wqup.shape), _resident((1, KV_LORA)), _resident(wkup.shape),
            _resident(wvupt.shape),
            pl.BlockSpec((PROJ_TM, LANES), row), pl.BlockSpec((PROJ_TM, LANES), row),
            _resident(tri.shape), _resident(place.shape), _resident(qbias.shape),
            slab(w_o), slab(w_ff1), slab(w_ff2),
        ],
        out_specs=[
            pl.BlockSpec((PROJ_TM, 2 * GROUP_W), row),
            pl.BlockSpec((PROJ_TM, 2 * GROUP_W), row),
            pl.BlockSpec((PROJ_TM // ATT_TK, 2 * fw, ATT_TK), lambda i: (i, 0, 0)),
            slab(w_o), slab(w_ff1), slab(w_ff2),
        ],
        out_shape=[
            jax.ShapeDtypeStruct((seq, 2 * GROUP_W), BF16),
            jax.ShapeDtypeStruct((seq, 2 * GROUP_W), BF16),
            jax.ShapeDtypeStruct((n_kv, 2 * fw, ATT_TK), BF16),
            jax.ShapeDtypeStruct(w_o.shape, BF16),
            jax.ShapeDtypeStruct(w_ff1.shape, BF16),
            jax.ShapeDtypeStruct(w_ff2.shape, BF16),
        ],
        scratch_shapes=[pltpu.VMEM((8, LANES), F32)],
        compiler_params=pltpu.CompilerParams(
            dimension_semantics=("arbitrary",), vmem_limit_bytes=48 * 1024 * 1024),
        name="proj",
    )(x2, g_mix.reshape(1, d), win, wvt, bf, g_q.reshape(1, Q_LORA), wqup, g_kv.reshape(1, KV_LORA),
      wkup, wvupt, cos_t, sin_t, tri, place, qbias, w_o, w_ff1, w_ff2)

    n_pairs = N_HEADS
    o_all = pl.pallas_call(
        _attn_kernel,
        grid=(n_pairs, seq // ATT_TQ),
        in_specs=[
            pl.BlockSpec((ATT_TQ, 2 * LANES), lambda p, i: (i, p)),
            pl.BlockSpec((seq, 2 * LANES), lambda p, i: (0, p)),
            pl.BlockSpec((n_kv, LANES, ATT_TK), lambda p, i: (0, p, 0)),
            pl.BlockSpec((1, ATT_TK, ATT_TQ), lambda p, i: (p // (N_HEADS // 2), 0, 0)),
        ],
        out_specs=pl.BlockSpec((LANES, ATT_TQ), lambda p, i: (p, i)),
        out_shape=jax.ShapeDtypeStruct((2 * fw, seq), BF16),
        scratch_shapes=[
            pltpu.VMEM((2, 2, ATT_TK, ATT_TQ), BF16),
            pltpu.VMEM((2, 2, 1, ATT_TQ), F32),
            pltpu.VMEM((2, 2, 1, ATT_TQ), F32),
            pltpu.VMEM((2, LANES, ATT_TQ), F32),
            pltpu.VMEM((2, 1, ATT_TQ), F32),
            pltpu.VMEM((2, 1, ATT_TQ), F32),
            pltpu.VMEM((2, 1, ATT_TQ), F32),
            pltpu.SMEM((1,), F32),
        ],
        compiler_params=pltpu.CompilerParams(
            dimension_semantics=("arbitrary", "arbitrary"), vmem_limit_bytes=48 * 1024 * 1024),
        name="attn",
    )(q_all, k_all, vt_all, _diagonal_masks())

    return pl.pallas_call(
        _out_kernel,
        grid=(seq // OUT_TM,),
        in_specs=[
            pl.BlockSpec((2 * fw, OUT_TM), lambda i: (0, i)),
            pl.BlockSpec((OUT_TM, d), row),
            _resident((2 * fw, d)), _resident((1, d)), _resident((d, d_ff)), _resident((d_ff, d)),
            _resident((1, d)),
        ],
        out_specs=pl.BlockSpec((OUT_TM, d), row),
        out_shape=jax.ShapeDtypeStruct((seq, d), F32),
        compiler_params=pltpu.CompilerParams(
            dimension_semantics=("arbitrary",), vmem_limit_bytes=56 * 1024 * 1024),
        name="out_mlp",
    )(o_all, x2, wo_b, g_mlp.reshape(1, d), w1_b, w2_b, g_out.reshape(1, d))


def kernel(x, g_mix, w_in, b_f, g_q, w_q_up, g_kv, w_kv_up, w_o, g_mlp, w_ff1, w_ff2, g_final):
    b, seq, d = x.shape
    depth = w_in.shape[0]
    assert b == 1 and depth == 1, "single-sequence, single-layer problem"
    n_steps = seq // PROJ_TM
    assert all(w.shape[1] % (16 * n_steps) == 0 for w in (w_o, w_ff1, w_ff2)), "weight slabs must be bf16-tile aligned"
    assert seq % PROJ_TM == 0 and seq % ATT_TQ == 0 and seq % OUT_TM == 0 and ATT_TQ % ATT_TK == 0
    out = _layer(x[0], g_mix[0], w_in[0], b_f[0], g_q[0], w_q_up[0], g_kv[0], w_kv_up[0], w_o[0],
                 g_mlp[0], w_ff1[0], w_ff2[0], g_final)
    return out[None]
```
